```python
import math
import jax, jax.numpy as jnp
from jax import lax
import numpy as np

D_MODEL = 1024
BATCH = 8
SEQ = 8192
DEPTH = 1

MLA_HEADS = 8
Q_LORA = 256
KV_LORA = 128
QK_NOPE = 64
QK_ROPE = 32
V_HEAD = 64
QK_HEAD = QK_NOPE + QK_ROPE
ROPE_THETA = 10000.0
Q_BLOCK = 128
M_HEADS = 4
M_HEAD = 128
M_WIDTH = M_HEADS * M_HEAD
CONV_K = 5
CHUNK = 128
FFN_DIM = 2816
EPS = 1e-6
IN_SPLITS = (Q_LORA, KV_LORA, QK_ROPE, M_WIDTH, M_WIDTH, 4 * M_HEADS, D_MODEL, D_MODEL)
IN_DIM = Q_LORA + KV_LORA + QK_ROPE + 2 * M_WIDTH + 4 * M_HEADS + 2 * D_MODEL

kernel_name = "hybrid_mla_mlstm_macaron_block"


def rmsnorm(x, w):
    xf = x.astype(jnp.float32)
    y = xf * lax.rsqrt(jnp.mean(xf * xf, axis=-1, keepdims=True) + EPS)
    return (y * w.astype(jnp.float32)).astype(x.dtype)


def swiglu(x, w_gate, w_up, w_down):
    return (jax.nn.silu(x @ w_gate) * (x @ w_up)) @ w_down


def rope_tables(positions):
    half = QK_ROPE // 2
    inv = ROPE_THETA ** (-jnp.arange(half, dtype=jnp.float32) / half)
    ang = positions.astype(jnp.float32)[..., None] * inv
    return jnp.cos(ang)[:, :, None, :], jnp.sin(ang)[:, :, None, :]


def apply_rope(x, cos, sin):
    half = QK_ROPE // 2
    x1, x2 = x[..., :half], x[..., half:]
    c, s = cos.astype(x.dtype), sin.astype(x.dtype)
    return jnp.concatenate([x1 * c - x2 * s, x2 * c + x1 * s], axis=-1)


def block_attention(q, k, v):
    B, H, S, dq = q.shape
    nb = S // Q_BLOCK
    qb = q.reshape(B, H, nb, Q_BLOCK, dq).transpose(2, 0, 1, 3, 4)
    scale = QK_HEAD ** -0.5

    def one(qblk):
        s = jnp.einsum('bhqd,bhkd->bhqk', qblk, k).astype(jnp.float32) * scale
        p = jax.nn.softmax(s, axis=-1)
        return jnp.einsum('bhqk,bhkd->bhqd', p.astype(v.dtype), v)

    o = lax.map(one, qb)
    return o.transpose(1, 0, 3, 2, 4).reshape(B, S, H * v.shape[-1])


def mla_branch(c_q, c_kv, k_pe, positions, q_a_norm_w, w_uq, kv_a_norm_w, w_uk, w_uv,
               q_norm_w, k_norm_w):
    B, S, _ = c_q.shape
    q = (rmsnorm(c_q, q_a_norm_w) @ w_uq).reshape(B, S, MLA_HEADS, QK_HEAD)
    ckv = rmsnorm(c_kv, kv_a_norm_w)
    k_nope = (ckv @ w_uk).reshape(B, S, MLA_HEADS, QK_NOPE)
    v = (ckv @ w_uv).reshape(B, S, MLA_HEADS, V_HEAD)
    k = jnp.concatenate(
        [k_nope, jnp.broadcast_to(k_pe[:, :, None, :], (B, S, MLA_HEADS, QK_ROPE))], axis=-1)
    q = rmsnorm(q, q_norm_w)
    k = rmsnorm(k, k_norm_w)
    cos, sin = rope_tables(positions)
    q = jnp.concatenate([q[..., :QK_NOPE], apply_rope(q[..., QK_NOPE:], cos, sin)], axis=-1)
    k = jnp.concatenate([k[..., :QK_NOPE], apply_rope(k[..., QK_NOPE:], cos, sin)], axis=-1)
    return block_attention(q.transpose(0, 2, 1, 3), k.transpose(0, 2, 1, 3),
                           v.transpose(0, 2, 1, 3))


def centred_conv(x, w, b):
    C = x.shape[-1]
    y = lax.conv_general_dilated(
        x, w.astype(x.dtype)[:, None, :], window_strides=(1,),
        padding=[(CONV_K // 2, CONV_K // 2)], dimension_numbers=('NWC', 'WIO', 'NWC'),
        feature_group_count=C)
    return y + b


def mlstm_scan(q, k, v, i_pre, logf):
    B, H, S, dk = q.shape
    dv = v.shape[-1]
    nc = S // CHUNK
    to_chunks = lambda t: jnp.moveaxis(t.reshape(B, H, nc, CHUNK, *t.shape[3:]), 2, 0)
    mask = jnp.tril(jnp.ones((CHUNK, CHUNK), dtype=bool))

    def step(carry, inp):
        C, n, m = carry
        qc, kc, vc, ic, fc = inp
        b = jnp.cumsum(fc, axis=-1)
        logw = b[..., :, None] - b[..., None, :] + ic[..., None, :]
        logw = jnp.where(mask, logw, -jnp.inf)
        inter = b + m[..., None]
        m_t = jnp.maximum(jnp.max(logw, axis=-1), inter)
        sc = jnp.einsum('bhtd,bhsd->bhts', qc, kc) * jnp.exp(logw - m_t[..., None])
        inter_w = jnp.exp(inter - m_t)
        num = jnp.einsum('bhts,bhsv->bhtv', sc, vc) + inter_w[..., None] * jnp.einsum('bhtk,bhkv->bhtv', qc, C)
        den = jnp.sum(sc, axis=-1) + inter_w * jnp.einsum('bhtk,bhk->bht', qc, n)
        h = num / jnp.maximum(jnp.abs(den), jnp.exp(-m_t))[..., None]
        bL = b[..., -1]
        g = bL[..., None] - b + ic
        m_new = jnp.maximum(bL + m, jnp.max(g, axis=-1))
        decay = jnp.exp(bL + m - m_new)
        ws = jnp.exp(g - m_new[..., None])
        C_new = decay[..., None, None] * C + jnp.einsum('bhs,bhsk,bhsv->bhkv', ws, kc, vc)
        n_new = decay[..., None] * n + jnp.einsum('bhs,bhsk->bhk', ws, kc)
        return (C_new, n_new, m_new), h

    init = (jnp.zeros((B, H, dk, dv), jnp.float32), jnp.zeros((B, H, dk), jnp.float32),
            jnp.zeros((B, H), jnp.float32))
    _, hs = lax.scan(step, init, (to_chunks(q), to_chunks(k), to_chunks(v),
                                  to_chunks(i_pre), to_chunks(logf)))
    return jnp.moveaxis(hs, 0, 2).reshape(B, H, S, dv)


def mlstm_branch(m_in, o_pre, gates, conv_w, conv_b, w_mq, w_mk, w_mv, b_igate, b_fgate,
                 m_norm_w, m_skip):
    B, S, _ = m_in.shape
    xc = jax.nn.silu(centred_conv(m_in, conv_w, conv_b))
    xch = xc.reshape(B, S, M_HEADS, M_HEAD)
    xmh = m_in.reshape(B, S, M_HEADS, M_HEAD)
    f32 = jnp.float32
    q = jnp.einsum('bshd,hde->bhse', xch, w_mq).astype(f32)
    k = (jnp.einsum('bshd,hde->bhse', xch, w_mk) * (M_HEAD ** -0.5)).astype(f32)
    v = jnp.einsum('bshd,hde->bhse', xmh, w_mv).astype(f32)
    gf = gates.astype(f32)
    i_pre = (gf[..., :2 * M_HEADS] + b_igate.astype(f32)).transpose(0, 2, 1)
    logf = jax.nn.log_sigmoid(gf[..., 2 * M_HEADS:] + b_fgate.astype(f32)).transpose(0, 2, 1)
    flip = lambda t: jnp.flip(t, axis=2)
    h_fwd = mlstm_scan(q, k, v, i_pre[:, :M_HEADS], logf[:, :M_HEADS])
    h_bwd = flip(mlstm_scan(flip(q), flip(k), flip(v), flip(i_pre[:, M_HEADS:]),
                            flip(logf[:, M_HEADS:])))
    h_cell = (h_fwd + h_bwd).transpose(0, 2, 1, 3)
    hn = rmsnorm(h_cell, m_norm_w.reshape(M_HEADS, M_HEAD)).reshape(B, S, M_WIDTH)
    return jax.nn.sigmoid(o_pre) * (hn.astype(m_in.dtype) + m_skip * xc)


def setup_inputs(seed: int = 0) -> dict:
    key = jax.random.key(seed)
    ks = iter(jax.random.split(key, 48))
    f32 = jnp.float32
    nrm = lambda shape, fan_in: jax.random.normal(next(ks), shape, f32) * fan_in ** -0.5
    gain = lambda shape: 1.0 + 0.02 * jax.random.normal(next(ks), shape, f32)
    L = DEPTH
    x = jax.random.normal(next(ks), (BATCH, SEQ, D_MODEL), f32)
    offs = jax.random.randint(next(ks), (BATCH, 1), 0, 1024, dtype=jnp.int32)
    positions = jnp.arange(SEQ, dtype=jnp.int32)[None, :] + offs
    f_bias = jnp.tile(jnp.linspace(3.0, 6.0, M_HEADS, dtype=f32), 2)[None, :]
    return {
        "x": x,
        "positions": positions,
        "ffn1_norm_w": gain((L, D_MODEL)),
        "ffn1_w_gate": nrm((L, D_MODEL, FFN_DIM), D_MODEL),
        "ffn1_w_up": nrm((L, D_MODEL, FFN_DIM), D_MODEL),
        "ffn1_w_down": nrm((L, FFN_DIM, D_MODEL), FFN_DIM),
        "mix_norm_w": gain((L, D_MODEL)),
        "w_in": nrm((L, D_MODEL, IN_DIM), D_MODEL),
        "q_a_norm_w": gain((L, Q_LORA)),
        "w_uq": nrm((L, Q_LORA, MLA_HEADS * QK_HEAD), Q_LORA),
        "kv_a_norm_w": gain((L, KV_LORA)),
        "w_uk": nrm((L, KV_LORA, MLA_HEADS * QK_NOPE), KV_LORA),
        "w_uv": nrm((L, KV_LORA, MLA_HEADS * V_HEAD), KV_LORA),
        "q_norm_w": gain((L, QK_HEAD)),
        "k_norm_w": gain((L, QK_HEAD)),
        "w_branch_a": nrm((L, MLA_HEADS * V_HEAD, D_MODEL), MLA_HEADS * V_HEAD),
        "conv_w": nrm((L, CONV_K, M_WIDTH), CONV_K),
        "conv_b": 0.02 * jax.random.normal(next(ks), (L, M_WIDTH), f32),
        "w_mq": nrm((L, M_HEADS, M_HEAD, M_HEAD), M_HEAD),
        "w_mk": nrm((L, M_HEADS, M_HEAD, M_HEAD), M_HEAD),
        "w_mv": nrm((L, M_HEADS, M_HEAD, M_HEAD), M_HEAD),
        "b_igate": 0.1 * jax.random.normal(next(ks), (L, 2 * M_HEADS), f32),
        "b_fgate": f_bias + 0.1 * jax.random.normal(next(ks), (L, 2 * M_HEADS), f32),
        "m_norm_w": gain((L, M_WIDTH)),
        "m_skip": gain((L, M_WIDTH)),
        "w_branch_b": nrm((L, M_WIDTH, D_MODEL), M_WIDTH),
        "w_out": nrm((L, D_MODEL, D_MODEL), D_MODEL),
        "ffn2_norm_w": gain((L, D_MODEL)),
        "ffn2_w_gate": nrm((L, D_MODEL, FFN_DIM), D_MODEL),
        "ffn2_w_up": nrm((L, D_MODEL, FFN_DIM), D_MODEL),
        "ffn2_w_down": nrm((L, FFN_DIM, D_MODEL), FFN_DIM),
        "final_norm_w": gain((L, D_MODEL)),
    }


def reference(x, positions, ffn1_norm_w, ffn1_w_gate, ffn1_w_up, ffn1_w_down, mix_norm_w, w_in,
              q_a_norm_w, w_uq, kv_a_norm_w, w_uk, w_uv, q_norm_w, k_norm_w, w_branch_a,
              conv_w, conv_b, w_mq, w_mk, w_mv, b_igate, b_fgate, m_norm_w, m_skip, w_branch_b,
              w_out, ffn2_norm_w, ffn2_w_gate, ffn2_w_up, ffn2_w_down, final_norm_w):
    split_points = [sum(IN_SPLITS[:j + 1]) for j in range(len(IN_SPLITS) - 1)]
    for l in range(DEPTH):
        h = rmsnorm(x, ffn1_norm_w[l])
        x = x + 0.5 * swiglu(h, ffn1_w_gate[l], ffn1_w_up[l], ffn1_w_down[l])
        h = rmsnorm(x, mix_norm_w[l])
        p = h @ w_in[l]
        c_q, c_kv, k_pe, m_in, o_pre, gates, g_a, g_b = jnp.split(p, split_points, axis=-1)
        a = mla_branch(c_q, c_kv, k_pe, positions, q_a_norm_w[l], w_uq[l], kv_a_norm_w[l],
                       w_uk[l], w_uv[l], q_norm_w[l], k_norm_w[l])
        bm = mlstm_branch(m_in, o_pre, gates, conv_w[l], conv_b[l], w_mq[l], w_mk[l], w_mv[l],
                          b_igate[l], b_fgate[l], m_norm_w[l], m_skip[l])
        merged = jax.nn.sigmoid(g_a) * (a @ w_branch_a[l]) + jax.nn.sigmoid(g_b) * (bm @ w_branch_b[l])
        x = x + merged @ w_out[l]
        h = rmsnorm(x, ffn2_norm_w[l])
        x = x + 0.5 * swiglu(h, ffn2_w_gate[l], ffn2_w_up[l], ffn2_w_down[l])
        x = rmsnorm(x, final_norm_w[l])
    return x
```

```python
import functools
import math

import jax
import jax.numpy as jnp
from jax import lax
from jax.experimental import pallas as pl
from jax.experimental.pallas import tpu as pltpu

F32 = jnp.float32
BF16 = jnp.bfloat16

EPS = 1e-6
ROPE_THETA = 10000.0
LANES = 128
MLA_HEADS = 8
QK_NOPE = 64
QK_ROPE = 32
QK_HEAD = QK_NOPE + QK_ROPE
V_HEAD = 64
VT_ROWS = 80
M_HEADS = 4
M_HEAD = 128
CONV_K = 5
CHUNK = 128
LAT_W = 512
GATE_W = 128
VMEM_LIMIT = 56 * 1024 * 1024

_NT = (((1,), (1,)), ((), ()))
_TN = (((0,), (0,)), ((), ()))


def _rms(x, w):
    ms = jnp.mean(x * x, axis=-1, keepdims=True)
    return x * lax.rsqrt(ms + EPS) * w


def _dot(a, b):
    return jnp.dot(a, b, preferred_element_type=F32)


def _resident(shape):
    nd = len(shape)
    return pl.BlockSpec(shape, lambda *_: (0,) * nd, pipeline_mode=pl.Buffered(1))


def _params(*sem):
    return pltpu.CompilerParams(dimension_semantics=sem, vmem_limit_bytes=VMEM_LIMIT)


def _ffn_kernel(x_ref, nw_ref, wg_ref, wu_ref, wd_ref, fw_ref, o_ref, *, final_norm):
    x = x_ref[...]
    h = _rms(x, nw_ref[...]).astype(BF16)
    g = _dot(h, wg_ref[...])
    u = _dot(h, wu_ref[...])
    a = (g * jax.nn.sigmoid(g) * u).astype(BF16)
    out = x + 0.5 * _dot(a, wd_ref[...])
    if final_norm:
        out = _rms(out, fw_ref[...])
    o_ref[...] = out


def _ffn(x, nw, wg, wu, wd, fw, *, final_norm, tm=512):
    t, d = x.shape
    f = wg.shape[1]
    row = pl.BlockSpec((tm, d), lambda i: (i, 0))
    return pl.pallas_call(
        functools.partial(_ffn_kernel, final_norm=final_norm),
        out_shape=jax.ShapeDtypeStruct((t, d), F32),
        grid=(t // tm,),
        in_specs=[row, _resident((1, d)), _resident((d, f)), _resident((d, f)),
                  _resident((f, d)), _resident((1, d))],
        out_specs=row,
        compiler_params=_params("parallel"),
        name="ffn",
    )(x, nw, wg, wu, wd, fw)


def _proj_kernel(x_ref, nw_ref, w_ref, lat_ref, min_ref, opre_ref, gates_ref, ga_ref, gb_ref, *, d, mw):
    h = _rms(x_ref[...], nw_ref[...]).astype(BF16)
    o = 0
    for ref, width in ((lat_ref, LAT_W), (min_ref, mw), (opre_ref, mw), (gates_ref, GATE_W),
                       (ga_ref, d), (gb_ref, d)):
        ref[...] = _dot(h, w_ref[:, o:o + width]).astype(ref.dtype)
        o += width


def _proj(x, nw, wcat, *, mw, tm=512):
    t, d = x.shape
    n = wcat.shape[1]
    row = lambda w: pl.BlockSpec((tm, w), lambda i: (i, 0))
    out_shape = (jax.ShapeDtypeStruct((t, LAT_W), F32), jax.ShapeDtypeStruct((t, mw), F32),
                 jax.ShapeDtypeStruct((t, mw), BF16), jax.ShapeDtypeStruct((t, GATE_W), F32),
                 jax.ShapeDtypeStruct((t, d), BF16), jax.ShapeDtypeStruct((t, d), BF16))
    return pl.pallas_call(
        functools.partial(_proj_kernel, d=d, mw=mw),
        out_shape=out_shape,
        grid=(t // tm,),
        in_specs=[row(d), _resident((1, d)), _resident((d, n))],
        out_specs=(row(LAT_W), row(mw), row(mw), row(GATE_W), row(d), row(d)),
        compiler_params=_params("parallel"),
        name="proj",
    )(x, nw, wcat)


def _mla_prep_kernel(lat_ref, pos_ref, qaw_ref, wuq_ref, kvaw_ref, wuk_ref, wuvt_ref, qnw_ref, knw_ref,
                     invl_ref, sgn_ref, q_ref, k_ref, vt_ref, *, q_lora, kv_lora, q_scale):
    tm = lat_ref.shape[0]
    lat = lat_ref[...]
    cqn = _rms(lat[:, 0:q_lora], qaw_ref[...]).astype(BF16)
    ckvn = _rms(lat[:, q_lora:q_lora + kv_lora], kvaw_ref[...]).astype(BF16)
    kpe = pltpu.roll(lat[:, q_lora + kv_lora:LAT_W], QK_NOPE, axis=1)
    qf = _dot(cqn, wuq_ref[...])
    kf = _dot(ckvn, wuk_ref[...])
    vt = lax.dot_general(wuvt_ref[...], ckvn, _NT, preferred_element_type=F32)

    ang = pos_ref[...].astype(F32) * invl_ref[...]
    cos = jnp.cos(ang)
    sin = jnp.sin(ang) * sgn_ref[...]
    lane = lax.broadcasted_iota(jnp.int32, (tm, LANES), 1)
    first_half = lane < QK_NOPE + QK_ROPE // 2

    def norm_rope(x, w):
        ms = jnp.sum(x * x, axis=-1, keepdims=True) * (1.0 / QK_HEAD)
        y = x * lax.rsqrt(ms + EPS) * w
        partner = jnp.where(first_half, pltpu.roll(y, LANES - QK_ROPE // 2, axis=1),
                            pltpu.roll(y, QK_ROPE // 2, axis=1))
        return y * cos + partner * sin

    ones_row = jnp.where(lax.broadcasted_iota(jnp.int32, (VT_ROWS - V_HEAD, tm), 0) == 0, 1.0, 0.0).astype(BF16)
    for h in range(MLA_HEADS):
        sl = slice(h * LANES, (h + 1) * LANES)
        q_ref[0, h] = (norm_rope(qf[:, sl], qnw_ref[...]) * q_scale).astype(BF16)
        k_ref[0, h] = norm_rope(kf[:, sl] + kpe, knw_ref[...]).astype(BF16)
        vt_ref[0, h, 0:V_HEAD, :] = vt[h * V_HEAD:(h + 1) * V_HEAD, :].astype(BF16)
        vt_ref[0, h, V_HEAD:VT_ROWS, :] = ones_row


def _mla_prep(lat, pos, qaw, wuq, kvaw, wuk, wuvt, qnw, knw, invl, sgn, *, b, s, q_lora, kv_lora, q_scale,
              tm=512):
    nt = s // tm
    hq = MLA_HEADS
    row = lambda w: pl.BlockSpec((tm, w), lambda bi, i: (bi * nt + i, 0))
    qk_spec = pl.BlockSpec((1, hq, tm, LANES), lambda bi, i: (bi, 0, i, 0))
    in_specs = [row(LAT_W), row(1)] + [_resident(a.shape) for a in
                                       (qaw, wuq, kvaw, wuk, wuvt, qnw, knw, invl, sgn)]
    return pl.pallas_call(
        functools.partial(_mla_prep_kernel, q_lora=q_lora, kv_lora=kv_lora, q_scale=q_scale),
        out_shape=(jax.ShapeDtypeStruct((b, hq, s, LANES), BF16), jax.ShapeDtypeStruct((b, hq, s, LANES), BF16),
                   jax.ShapeDtypeStruct((b, hq, VT_ROWS, s), BF16)),
        grid=(b, nt),
        in_specs=in_specs,
        out_specs=(qk_spec, qk_spec, pl.BlockSpec((1, hq, VT_ROWS, tm), lambda bi, i: (bi, 0, 0, i))),
        compiler_params=_params("parallel", "parallel"),
        name="mla_prep",
    )(lat, pos, qaw, wuq, kvaw, wuk, wuvt, qnw, knw, invl, sgn)


def _attn_kernel(q_ref, k_ref, vt_ref, o_ref, s_scr, *, tk):
    tq = q_ref.shape[2]
    nk = k_ref.shape[2] // tk
    q = q_ref[0, 0]

    def scores(c, m):
        ks = pl.ds(pl.multiple_of(c * tk, tk), tk)
        st = lax.dot_general(k_ref[0, 0, ks, :], q, _NT, preferred_element_type=F32)
        s_scr[ks, :] = st
        return jnp.maximum(m, jnp.max(st, axis=0, keepdims=True))

    m = lax.fori_loop(0, nk, scores, jnp.full((1, tq), -jnp.inf, F32))

    def values(c, acc):
        ks = pl.ds(pl.multiple_of(c * tk, tk), tk)
        p = jnp.exp2(s_scr[ks, :] - m).astype(BF16)
        return acc + _dot(vt_ref[0, 0, :, ks], p)

    acc = lax.fori_loop(0, nk, values, jnp.zeros((VT_ROWS, tq), F32))
    o_ref[0, 0] = (acc[0:V_HEAD] / acc[V_HEAD:V_HEAD + 1]).astype(o_ref.dtype)


def _attn(q, k, vt, *, tq=256, tk=512):
    b, h, s, _ = q.shape
    return pl.pallas_call(
        functools.partial(_attn_kernel, tk=tk),
        out_shape=jax.ShapeDtypeStruct((b, h, V_HEAD, s), BF16),
        grid=(b, h, s // tq),
        in_specs=[pl.BlockSpec((1, 1, tq, LANES), lambda bi, hi, i: (bi, hi, i, 0)),
                  pl.BlockSpec((1, 1, s, LANES), lambda bi, hi, i: (bi, hi, 0, 0)),
                  pl.BlockSpec((1, 1, VT_ROWS, s), lambda bi, hi, i: (bi, hi, 0, 0))],
        out_specs=pl.BlockSpec((1, 1, V_HEAD, tq), lambda bi, hi, i: (bi, hi, 0, i)),
        scratch_shapes=[pltpu.VMEM((s, tq), F32)],
        compiler_params=_params("parallel", "parallel", "arbitrary"),
        name="attn",
    )(q, k, vt)


HALO = 8


def _mprep_kernel(cur_ref, prev_ref, next_ref, cw_ref, cb_ref, wq_ref, wk_ref, wv_ref,
                  xc_ref, q_ref, k_ref, v_ref, xs_ref):
    tm = cur_ref.shape[0]
    i = pl.program_id(1)
    cur = cur_ref[...]
    xs_ref[0:HALO, :] = jnp.where(i > 0, prev_ref[...], 0.0)
    xs_ref[HALO:HALO + tm, :] = cur
    xs_ref[HALO + tm:2 * HALO + tm, :] = jnp.where(i < pl.num_programs(1) - 1, next_ref[...], 0.0)
    y = cb_ref[...]
    for kk in range(CONV_K):
        y = y + cw_ref[kk:kk + 1, :] * xs_ref[pl.ds(HALO - CONV_K // 2 + kk, tm), :]
    xc = (y * jax.nn.sigmoid(y)).astype(BF16)
    xc_ref[...] = xc
    xm = cur.astype(BF16)
    for h in range(M_HEADS):
        sl = slice(h * M_HEAD, (h + 1) * M_HEAD)
        q_ref[:, sl] = _dot(xc[:, sl], wq_ref[h]).astype(BF16)
        k_ref[:, sl] = (_dot(xc[:, sl], wk_ref[h]) * (M_HEAD ** -0.5)).astype(BF16)
        v_ref[:, sl] = _dot(xm[:, sl], wv_ref[h]).astype(BF16)


def _mprep(m_in, cw, cb, wq, wk, wv, *, b, s, tm=512):
    t, mw = m_in.shape
    nt = s // tm
    hb = tm // HALO
    row = pl.BlockSpec((tm, mw), lambda bi, i: (bi * nt + i, 0))
    prev = pl.BlockSpec((HALO, mw), lambda bi, i: (jnp.maximum((bi * nt + i) * hb - 1, 0), 0))
    nxt = pl.BlockSpec((HALO, mw), lambda bi, i: (jnp.minimum((bi * nt + i + 1) * hb, t // HALO - 1), 0))
    out = jax.ShapeDtypeStruct((t, mw), BF16)
    return pl.pallas_call(
        _mprep_kernel,
        out_shape=(out, out, out, out),
        grid=(b, nt),
        in_specs=[row, prev, nxt] + [_resident(a.shape) for a in (cw, cb, wq, wk, wv)],
        out_specs=(row, row, row, row),
        scratch_shapes=[pltpu.VMEM((tm + 2 * HALO, mw), F32)],
        compiler_params=_params("parallel", "parallel"),
        name="m_prep",
    )(m_in, m_in, m_in, cw, cb, wq, wk, wv)


def _log_sigmoid(x):
    return jnp.minimum(x, 0.0) - jnp.log1p(jnp.exp(-jnp.abs(x)))


def _cumsum_rows(tri, x):
    hi = x.astype(BF16)
    r1 = x - hi.astype(F32)
    mid = r1.astype(BF16)
    lo = (r1 - mid.astype(F32)).astype(BF16)
    return _dot(tri, hi) + _dot(tri, mid) + _dot(tri, lo)


def _mscan_kernel(qf_ref, kf_ref, vf_ref, gf_ref, qb_ref, kb_ref, vb_ref, gb_ref, bias_ref,
                  hf_ref, hb_ref, c_scr, m_scr, *, sub):
    L = CHUNK

    @pl.when(pl.program_id(1) == 0)
    def _():
        c_scr[...] = jnp.zeros_like(c_scr)
        m_scr[...] = jnp.zeros_like(m_scr)

    row = lax.broadcasted_iota(jnp.int32, (L, L), 0)
    col = lax.broadcasted_iota(jnp.int32, (L, L), 1)
    causal = (col <= row, col >= row)
    tri = tuple(jnp.where(c, 1.0, 0.0).astype(BF16) for c in causal)
    ones_col = jnp.where(col == 0, 1.0, 0.0).astype(BF16)
    last = (L - 1, 0)
    dirs = ((qf_ref, kf_ref, vf_ref, gf_ref, hf_ref), (qb_ref, kb_ref, vb_ref, gb_ref, hb_ref))

    for c in range(sub):
        for d, (q_ref, k_ref, v_ref, g_ref, h_ref) in enumerate(dirs):
            rows = slice((c if d == 0 else sub - 1 - c) * L, (c if d == 0 else sub - 1 - c) * L + L)
            gg = g_ref[rows, :] + bias_ref[...]
            bc = _cumsum_rows(tri[d], _log_sigmoid(gg))
            r = gg - pltpu.roll(bc, GATE_W - 2 * M_HEADS, axis=1)
            rt = r.T
            for h in range(M_HEADS):
                ch = d * M_HEADS + h
                hs = slice(h * M_HEAD, (h + 1) * M_HEAD)
                qc, kc = q_ref[rows, hs], k_ref[rows, hs]
                v_ext = jnp.concatenate([v_ref[rows, hs], ones_col], axis=1)
                lf = 2 * M_HEADS + ch
                b_col = bc[:, lf:lf + 1]
                r_col = r[:, ch:ch + 1]
                b_tot = bc[last[d]:last[d] + 1, lf:lf + 1]
                m = m_scr[ch, 0:1, 0:1]
                cst = c_scr[ch]

                logw = jnp.where(causal[d], b_col + rt[ch:ch + 1, :], -jnp.inf)
                inter = b_col + m
                m_t = jnp.maximum(jnp.max(logw, axis=1, keepdims=True), inter)
                qk = lax.dot_general(qc, kc, _NT, preferred_element_type=F32)
                sc = (qk * jnp.exp(logw - m_t)).astype(BF16)
                tot = _dot(sc, v_ext) + jnp.exp(inter - m_t) * _dot(qc, cst.astype(BF16))
                den = jnp.maximum(jnp.abs(tot[:, M_HEAD:M_HEAD + 1]), jnp.exp(-m_t))
                h_ref[rows, hs] = (tot[:, 0:M_HEAD] / den).astype(h_ref.dtype)

                g_col = b_tot + r_col
                m_new = jnp.maximum(b_tot + m, jnp.max(g_col, axis=0, keepdims=True))
                kw = (kc.astype(F32) * jnp.exp(g_col - m_new)).astype(BF16)
                c_scr[ch] = jnp.exp(b_tot + m - m_new) * cst + lax.dot_general(
                    kw, v_ext, _TN, preferred_element_type=F32)
                m_scr[ch] = jnp.broadcast_to(m_new, m_scr.shape[1:])


def _mscan(q, k, v, gates, bias, *, b, s, sub=4):
    t, mw = q.shape
    blk = sub * CHUNK
    nb = s // blk
    fwd = lambda w: pl.BlockSpec((blk, w), lambda bi, j: (bi * nb + j, 0))
    bwd = lambda w: pl.BlockSpec((blk, w), lambda bi, j: (bi * nb + nb - 1 - j, 0))
    out = jax.ShapeDtypeStruct((t, mw), BF16)
    return pl.pallas_call(
        functools.partial(_mscan_kernel, sub=sub),
        out_shape=(out, out),
        grid=(b, nb),
        in_specs=[fwd(mw), fwd(mw), fwd(mw), fwd(GATE_W), bwd(mw), bwd(mw), bwd(mw), bwd(GATE_W),
                  _resident((1, GATE_W))],
        out_specs=(fwd(mw), bwd(mw)),
        scratch_shapes=[pltpu.VMEM((2 * M_HEADS, M_HEAD, 2 * M_HEAD), F32),
                        pltpu.VMEM((2 * M_HEADS, 8, LANES), F32)],
        compiler_params=_params("parallel", "arbitrary"),
        name="m_scan",
    )(q, k, v, gates, q, k, v, gates, bias)


def _merge_kernel(x_ref, at_ref, hf_ref, hb_ref, xc_ref, opre_ref, ga_ref, gb_ref,
                  mnw_ref, skip_ref, wa_ref, wb_ref, wo_ref, o_ref):
    hc = hf_ref[...].astype(F32) + hb_ref[...].astype(F32)
    hn = jnp.concatenate(
        [_rms(hc[:, h * M_HEAD:(h + 1) * M_HEAD], mnw_ref[:, h * M_HEAD:(h + 1) * M_HEAD])
         for h in range(M_HEADS)], axis=1)
    bm = jax.nn.sigmoid(opre_ref[...].astype(F32)) * (hn + skip_ref[...] * xc_ref[...].astype(F32))
    ya = lax.dot_general(at_ref[0], wa_ref[...], _TN, preferred_element_type=F32)
    yb = _dot(bm.astype(BF16), wb_ref[...])
    merged = (jax.nn.sigmoid(ga_ref[...].astype(F32)) * ya + jax.nn.sigmoid(gb_ref[...].astype(F32)) * yb)
    o_ref[...] = x_ref[...] + _dot(merged.astype(BF16), wo_ref[...])


def _merge(x, at, hf, hb, xc, opre, ga, gb, mnw, skip, wa, wb, wo, *, b, s, tm=512):
    t, d = x.shape
    mw = hf.shape[1]
    nt = s // tm
    row = lambda w: pl.BlockSpec((tm, w), lambda bi, i: (bi * nt + i, 0))
    return pl.pallas_call(
        _merge_kernel,
        out_shape=jax.ShapeDtypeStruct((t, d), F32),
        grid=(b, nt),
        in_specs=[row(d), pl.BlockSpec((1, at.shape[1], tm), lambda bi, i: (bi, 0, i)),
                  row(mw), row(mw), row(mw), row(mw), row(d), row(d)]
                 + [_resident(a.shape) for a in (mnw, skip, wa, wb, wo)],
        out_specs=row(d),
        compiler_params=_params("parallel", "parallel"),
        name="merge",
    )(x, at, hf, hb, xc, opre, ga, gb, mnw, skip, wa, wb, wo)


def _pad_cols(w, width):
    return jnp.pad(w, ((0, 0), (0, width - w.shape[1])))


def _head_pad(w, head_w):
    r = w.shape[0]
    w = w.reshape(r, MLA_HEADS, head_w)
    return jnp.pad(w, ((0, 0), (0, 0), (0, LANES - head_w))).reshape(r, MLA_HEADS * LANES)


def kernel(x, positions, ffn1_norm_w, ffn1_w_gate, ffn1_w_up, ffn1_w_down, mix_norm_w, w_in, q_a_norm_w, w_uq, kv_a_norm_w, w_uk, w_uv, q_norm_w, k_norm_w, w_branch_a, conv_w, conv_b, w_mq, w_mk, w_mv, b_igate, b_fgate, m_norm_w, m_skip, w_branch_b, w_out, ffn2_norm_w, ffn2_w_gate, ffn2_w_up, ffn2_w_down, final_norm_w):
    b, s, d = x.shape
    depth = w_in.shape[0]
    q_lora, kv_lora = q_a_norm_w.shape[1], kv_a_norm_w.shape[1]
    mw = m_norm_w.shape[1]
    bf = lambda a: a.astype(BF16)
    row = lambda a: a.reshape(1, -1).astype(F32)

    half = QK_ROPE // 2
    inv = ROPE_THETA ** (-jnp.arange(half, dtype=F32) / half)
    zeros = lambda n: jnp.zeros((n,), F32)
    invl = jnp.concatenate([zeros(QK_NOPE), inv, inv, zeros(LANES - QK_HEAD)]).reshape(1, LANES)
    sgn = jnp.concatenate([zeros(QK_NOPE), -jnp.ones((half,), F32), jnp.ones((half,), F32),
                           zeros(LANES - QK_HEAD)]).reshape(1, LANES)
    q_scale = QK_HEAD ** -0.5 * math.log2(math.e)

    xt = x.reshape(b * s, d)
    pos = positions.reshape(b * s, 1)
    for l in range(depth):
        xt = _ffn(xt, row(ffn1_norm_w[l]), bf(ffn1_w_gate[l]), bf(ffn1_w_up[l]), bf(ffn1_w_down[l]),
                  row(final_norm_w[l]), final_norm=False)

        w = w_in[l]
        o_lat = q_lora + kv_lora + QK_ROPE
        o_g = o_lat + 2 * mw
        o_ga = o_g + 4 * M_HEADS
        wcat = jnp.concatenate(
            [_pad_cols(w[:, :o_lat], LAT_W), w[:, o_lat:o_lat + mw], w[:, o_lat + mw:o_g],
             _pad_cols(w[:, o_g:o_ga], GATE_W), w[:, o_ga:o_ga + d], w[:, o_ga + d:]], axis=1)
        lat, m_in, o_pre, gates, g_a, g_b = _proj(xt, row(mix_norm_w[l]), bf(wcat), mw=mw)

        qh, kh, vt = _mla_prep(
            lat, pos, row(q_a_norm_w[l]), bf(_head_pad(w_uq[l], QK_HEAD)), row(kv_a_norm_w[l]),
            bf(_head_pad(w_uk[l], QK_NOPE)), bf(w_uv[l].T), _pad_cols(row(q_norm_w[l]), LANES),
            _pad_cols(row(k_norm_w[l]), LANES), invl, sgn,
            b=b, s=s, q_lora=q_lora, kv_lora=kv_lora, q_scale=q_scale)
        at = _attn(qh, kh, vt).reshape(b, MLA_HEADS * V_HEAD, s)

        xc, mq, mk, mv = _mprep(m_in, conv_w[l], row(conv_b[l]), bf(w_mq[l]), bf(w_mk[l]), bf(w_mv[l]),
                                b=b, s=s)
        gate_bias = _pad_cols(jnp.concatenate([row(b_igate[l]), row(b_fgate[l])], axis=1), GATE_W)
        hf, hb = _mscan(mq, mk, mv, gates, gate_bias, b=b, s=s)

        xt = _merge(xt, at, hf, hb, xc, o_pre, g_a, g_b, row(m_norm_w[l]), row(m_skip[l]),
                    bf(w_branch_a[l]), bf(w_branch_b[l]), bf(w_out[l]), b=b, s=s)
        xt = _ffn(xt, row(ffn2_norm_w[l]), bf(ffn2_w_gate[l]), bf(ffn2_w_up[l]), bf(ffn2_w_down[l]),
                  row(final_norm_w[l]), final_norm=True)
    return xt.reshape(b, s, d)
```

```python
import functools
import math

import jax
import jax.numpy as jnp
from jax import lax
from jax.experimental import pallas as pl
from jax.experimental.pallas import tpu as pltpu

F32 = jnp.float32
BF16 = jnp.bfloat16

EPS = 1e-6
ROPE_THETA = 10000.0
LANES = 128
MLA_HEADS = 8
QK_NOPE = 64
QK_ROPE = 32
QK_HEAD = QK_NOPE + QK_ROPE
V_HEAD = 64
VT_ROWS = 80
M_HEADS = 4
M_HEAD = 128
CONV_K = 5
CHUNK = 128
LAT_W = 512
GATE_W = 128
VMEM_LIMIT = 56 * 1024 * 1024

_NT = (((1,), (1,)), ((), ()))
_TN = (((0,), (0,)), ((), ()))


def _rms(x, w):
    ms = jnp.mean(x * x, axis=-1, keepdims=True)
    return x * lax.rsqrt(ms + EPS) * w


def _dot(a, b):
    return jnp.dot(a, b, preferred_element_type=F32)


def _resident(shape):
    nd = len(shape)
    return pl.BlockSpec(shape, lambda *_: (0,) * nd, pipeline_mode=pl.Buffered(1))


def _params(*sem):
    return pltpu.CompilerParams(dimension_semantics=sem, vmem_limit_bytes=VMEM_LIMIT)


def _ffn_kernel(x_ref, nw_ref, wg_ref, wu_ref, wd_ref, fw_ref, o_ref, *, final_norm):
    x = x_ref[...]
    h = _rms(x, nw_ref[...]).astype(BF16)
    g = _dot(h, wg_ref[...])
    u = _dot(h, wu_ref[...])
    a = (g * jax.nn.sigmoid(g) * u).astype(BF16)
    out = x + 0.5 * _dot(a, wd_ref[...])
    if final_norm:
        out = _rms(out, fw_ref[...])
    o_ref[...] = out


def _ffn(x, nw, wg, wu, wd, fw, *, final_norm, tm=512):
    t, d = x.shape
    f = wg.shape[1]
    row = pl.BlockSpec((tm, d), lambda i: (i, 0))
    return pl.pallas_call(
        functools.partial(_ffn_kernel, final_norm=final_norm),
        out_shape=jax.ShapeDtypeStruct((t, d), F32),
        grid=(t // tm,),
        in_specs=[row, _resident((1, d)), _resident((d, f)), _resident((d, f)),
                  _resident((f, d)), _resident((1, d))],
        out_specs=row,
        compiler_params=_params("parallel"),
        name="ffn",
    )(x, nw, wg, wu, wd, fw)


def _proj_kernel(x_ref, nw_ref, w_ref, lat_ref, min_ref, opre_ref, gates_ref, ga_ref, gb_ref, *, d, mw):
    h = _rms(x_ref[...], nw_ref[...]).astype(BF16)
    o = 0
    for ref, width in ((lat_ref, LAT_W), (min_ref, mw), (opre_ref, mw), (gates_ref, GATE_W),
                       (ga_ref, d), (gb_ref, d)):
        ref[...] = _dot(h, w_ref[:, o:o + width]).astype(ref.dtype)
        o += width


def _proj(x, nw, wcat, *, mw, tm=512):
    t, d = x.shape
    n = wcat.shape[1]
    row = lambda w: pl.BlockSpec((tm, w), lambda i: (i, 0))
    out_shape = (jax.ShapeDtypeStruct((t, LAT_W), F32), jax.ShapeDtypeStruct((t, mw), F32),
                 jax.ShapeDtypeStruct((t, mw), BF16), jax.ShapeDtypeStruct((t, GATE_W), F32),
                 jax.ShapeDtypeStruct((t, d), BF16), jax.ShapeDtypeStruct((t, d), BF16))
    return pl.pallas_call(
        functools.partial(_proj_kernel, d=d, mw=mw),
        out_shape=out_shape,
        grid=(t // tm,),
        in_specs=[row(d), _resident((1, d)), _resident((d, n))],
        out_specs=(row(LAT_W), row(mw), row(mw), row(GATE_W), row(d), row(d)),
        compiler_params=_params("parallel"),
        name="proj",
    )(x, nw, wcat)


def _mla_prep_kernel(lat_ref, pos_ref, qaw_ref, wuq_ref, kvaw_ref, wuk_ref, wuvt_ref, qnw_ref, knw_ref,
                     invl_ref, sgn_ref, q_ref, k_ref, vt_ref, *, q_lora, kv_lora, q_scale):
    tm = lat_ref.shape[0]
    lat = lat_ref[...]
    cqn = _rms(lat[:, 0:q_lora], qaw_ref[...]).astype(BF16)
    ckvn = _rms(lat[:, q_lora:q_lora + kv_lora], kvaw_ref[...]).astype(BF16)
    kpe = pltpu.roll(lat[:, q_lora + kv_lora:LAT_W], QK_NOPE, axis=1)
    qf = _dot(cqn, wuq_ref[...])
    kf = _dot(ckvn, wuk_ref[...])
    vt = lax.dot_general(wuvt_ref[...], ckvn, _NT, preferred_element_type=F32)

    ang = pos_ref[...].astype(F32) * invl_ref[...]
    cos = jnp.cos(ang)
    sin = jnp.sin(ang) * sgn_ref[...]
    lane = lax.broadcasted_iota(jnp.int32, (tm, LANES), 1)
    first_half = lane < QK_NOPE + QK_ROPE // 2

    def norm_rope(x, w):
        ms = jnp.sum(x * x, axis=-1, keepdims=True) * (1.0 / QK_HEAD)
        y = x * lax.rsqrt(ms + EPS) * w
        partner = jnp.where(first_half, pltpu.roll(y, LANES - QK_ROPE // 2, axis=1),
                            pltpu.roll(y, QK_ROPE // 2, axis=1))
        return y * cos + partner * sin

    ones_row = jnp.where(lax.broadcasted_iota(jnp.int32, (VT_ROWS - V_HEAD, tm), 0) == 0, 1.0, 0.0).astype(BF16)
    for h in range(MLA_HEADS):
        sl = slice(h * LANES, (h + 1) * LANES)
        q_ref[0, h] = (norm_rope(qf[:, sl], qnw_ref[...]) * q_scale).astype(BF16)
        k_ref[0, h] = norm_rope(kf[:, sl] + kpe, knw_ref[...]).astype(BF16)
        vt_ref[0, h, 0:V_HEAD, :] = vt[h * V_HEAD:(h + 1) * V_HEAD, :].astype(BF16)
        vt_ref[0, h, V_HEAD:VT_ROWS, :] = ones_row


def _mla_prep(lat, pos, qaw, wuq, kvaw, wuk, wuvt, qnw, knw, invl, sgn, *, b, s, q_lora, kv_lora, q_scale,
              tm=512):
    nt = s // tm
    hq = MLA_HEADS
    row = lambda w: pl.BlockSpec((tm, w), lambda bi, i: (bi * nt + i, 0))
    qk_spec = pl.BlockSpec((1, hq, tm, LANES), lambda bi, i: (bi, 0, i, 0))
    in_specs = [row(LAT_W), row(1)] + [_resident(a.shape) for a in
                                       (qaw, wuq, kvaw, wuk, wuvt, qnw, knw, invl, sgn)]
    return pl.pallas_call(
        functools.partial(_mla_prep_kernel, q_lora=q_lora, kv_lora=kv_lora, q_scale=q_scale),
        out_shape=(jax.ShapeDtypeStruct((b, hq, s, LANES), BF16), jax.ShapeDtypeStruct((b, hq, s, LANES), BF16),
                   jax.ShapeDtypeStruct((b, hq, VT_ROWS, s), BF16)),
        grid=(b, nt),
        in_specs=in_specs,
        out_specs=(qk_spec, qk_spec, pl.BlockSpec((1, hq, VT_ROWS, tm), lambda bi, i: (bi, 0, 0, i))),
        compiler_params=_params("parallel", "parallel"),
        name="mla_prep",
    )(lat, pos, qaw, wuq, kvaw, wuk, wuvt, qnw, knw, invl, sgn)


def _attn_kernel(q_ref, k_ref, vt_ref, o_ref, s0_scr, s1_scr, *, tk):
    tq = q_ref.shape[2]
    nk = k_ref.shape[2] // tk
    q = q_ref[0, 0]

    def chunk(c):
        return pl.ds(pl.multiple_of(c * tk, tk), tk)

    def scores(c, s_scr):
        st = lax.dot_general(k_ref[0, 0, chunk(c), :], q, _NT, preferred_element_type=F32)
        s_scr[...] = st
        return jnp.max(st, axis=0, keepdims=True)

    def values(c, s_scr, m, cm, acc):
        m_new = jnp.maximum(m, cm)
        p = jnp.exp2(s_scr[...] - m_new).astype(BF16)
        return m_new, acc * jnp.exp2(m - m_new) + _dot(vt_ref[0, 0, :, chunk(c)], p)

    def pair(i, carry):
        m, cm, acc = carry
        cm1 = scores(2 * i + 1, s1_scr)
        m, acc = values(2 * i, s0_scr, m, cm, acc)
        cm0 = scores(2 * i + 2, s0_scr)
        m, acc = values(2 * i + 1, s1_scr, m, cm1, acc)
        return m, cm0, acc

    init = (jnp.full((1, tq), -jnp.inf, F32), scores(0, s0_scr), jnp.zeros((VT_ROWS, tq), F32))
    m, cm, acc = lax.fori_loop(0, nk // 2 - 1, pair, init)
    cm1 = scores(nk - 1, s1_scr)
    m, acc = values(nk - 2, s0_scr, m, cm, acc)
    m, acc = values(nk - 1, s1_scr, m, cm1, acc)
    o_ref[0, 0] = (acc[0:V_HEAD] / acc[V_HEAD:V_HEAD + 1]).astype(o_ref.dtype)


def _attn(q, k, vt, *, tq=512, tk=1024):
    b, h, s, _ = q.shape
    assert s % (2 * tk) == 0 and s % tq == 0
    return pl.pallas_call(
        functools.partial(_attn_kernel, tk=tk),
        out_shape=jax.ShapeDtypeStruct((b, h, V_HEAD, s), BF16),
        grid=(b, h, s // tq),
        in_specs=[pl.BlockSpec((1, 1, tq, LANES), lambda bi, hi, i: (bi, hi, i, 0)),
                  pl.BlockSpec((1, 1, s, LANES), lambda bi, hi, i: (bi, hi, 0, 0)),
                  pl.BlockSpec((1, 1, VT_ROWS, s), lambda bi, hi, i: (bi, hi, 0, 0))],
        out_specs=pl.BlockSpec((1, 1, V_HEAD, tq), lambda bi, hi, i: (bi, hi, 0, i)),
        scratch_shapes=[pltpu.VMEM((tk, tq), F32), pltpu.VMEM((tk, tq), F32)],
        compiler_params=_params("parallel", "parallel", "arbitrary"),
        name="attn",
    )(q, k, vt)


HALO = 8


def _mprep_kernel(cur_ref, prev_ref, next_ref, cw_ref, cb_ref, wq_ref, wk_ref, wv_ref,
                  xc_ref, q_ref, k_ref, v_ref, xs_ref):
    tm = cur_ref.shape[0]
    i = pl.program_id(1)
    cur = cur_ref[...]
    xs_ref[0:HALO, :] = jnp.where(i > 0, prev_ref[...], 0.0)
    xs_ref[HALO:HALO + tm, :] = cur
    xs_ref[HALO + tm:2 * HALO + tm, :] = jnp.where(i < pl.num_programs(1) - 1, next_ref[...], 0.0)
    y = cb_ref[...]
    for kk in range(CONV_K):
        y = y + cw_ref[kk:kk + 1, :] * xs_ref[pl.ds(HALO - CONV_K // 2 + kk, tm), :]
    xc = (y * jax.nn.sigmoid(y)).astype(BF16)
    xc_ref[...] = xc
    xm = cur.astype(BF16)
    for h in range(M_HEADS):
        sl = slice(h * M_HEAD, (h + 1) * M_HEAD)
        q_ref[:, sl] = _dot(xc[:, sl], wq_ref[h]).astype(BF16)
        k_ref[:, sl] = (_dot(xc[:, sl], wk_ref[h]) * (M_HEAD ** -0.5)).astype(BF16)
        v_ref[:, sl] = _dot(xm[:, sl], wv_ref[h]).astype(BF16)


def _mprep(m_in, cw, cb, wq, wk, wv, *, b, s, tm=512):
    t, mw = m_in.shape
    nt = s // tm
    hb = tm // HALO
    row = pl.BlockSpec((tm, mw), lambda bi, i: (bi * nt + i, 0))
    prev = pl.BlockSpec((HALO, mw), lambda bi, i: (jnp.maximum((bi * nt + i) * hb - 1, 0), 0))
    nxt = pl.BlockSpec((HALO, mw), lambda bi, i: (jnp.minimum((bi * nt + i + 1) * hb, t // HALO - 1), 0))
    out = jax.ShapeDtypeStruct((t, mw), BF16)
    return pl.pallas_call(
        _mprep_kernel,
        out_shape=(out, out, out, out),
        grid=(b, nt),
        in_specs=[row, prev, nxt] + [_resident(a.shape) for a in (cw, cb, wq, wk, wv)],
        out_specs=(row, row, row, row),
        scratch_shapes=[pltpu.VMEM((tm + 2 * HALO, mw), F32)],
        compiler_params=_params("parallel", "parallel"),
        name="m_prep",
    )(m_in, m_in, m_in, cw, cb, wq, wk, wv)


def _log_sigmoid(x):
    return jnp.minimum(x, 0.0) - jnp.log1p(jnp.exp(-jnp.abs(x)))


def _cumsum_rows(tri, x):
    hi = x.astype(BF16)
    r1 = x - hi.astype(F32)
    mid = r1.astype(BF16)
    lo = (r1 - mid.astype(F32)).astype(BF16)
    return _dot(tri, hi) + _dot(tri, mid) + _dot(tri, lo)


def _mscan_kernel(qf_ref, kf_ref, vf_ref, gf_ref, qb_ref, kb_ref, vb_ref, gb_ref, bias_ref,
                  hf_ref, hb_ref, c_scr, m_scr, *, sub):
    L = CHUNK

    @pl.when(pl.program_id(1) == 0)
    def _():
        c_scr[...] = jnp.zeros_like(c_scr)
        m_scr[...] = jnp.zeros_like(m_scr)

    row = lax.broadcasted_iota(jnp.int32, (L, L), 0)
    col = lax.broadcasted_iota(jnp.int32, (L, L), 1)
    causal = (col <= row, col >= row)
    tri = tuple(jnp.where(c, 1.0, 0.0).astype(BF16) for c in causal)
    ones_col = jnp.where(col == 0, 1.0, 0.0).astype(BF16)
    last = (L - 1, 0)
    dirs = ((qf_ref, kf_ref, vf_ref, gf_ref, hf_ref), (qb_ref, kb_ref, vb_ref, gb_ref, hb_ref))

    for c in range(sub):
        for d, (q_ref, k_ref, v_ref, g_ref, h_ref) in enumerate(dirs):
            rows = slice((c if d == 0 else sub - 1 - c) * L, (c if d == 0 else sub - 1 - c) * L + L)
            gg = g_ref[rows, :] + bias_ref[...]
            bc = _cumsum_rows(tri[d], _log_sigmoid(gg))
            r = gg - pltpu.roll(bc, GATE_W - 2 * M_HEADS, axis=1)
            rt = r.T
            for h in range(M_HEADS):
                ch = d * M_HEADS + h
                hs = slice(h * M_HEAD, (h + 1) * M_HEAD)
                qc, kc = q_ref[rows, hs], k_ref[rows, hs]
                v_ext = jnp.concatenate([v_ref[rows, hs], ones_col], axis=1)
                lf = 2 * M_HEADS + ch
                b_col = bc[:, lf:lf + 1]
                r_col = r[:, ch:ch + 1]
                b_tot = bc[last[d]:last[d] + 1, lf:lf + 1]
                m = m_scr[ch, 0:1, 0:1]
                cst = c_scr[ch]

                logw = jnp.where(causal[d], b_col + rt[ch:ch + 1, :], -jnp.inf)
                inter = b_col + m
                m_t = jnp.maximum(jnp.max(logw, axis=1, keepdims=True), inter)
                qk = lax.dot_general(qc, kc, _NT, preferred_element_type=F32)
                sc = (qk * jnp.exp(logw - m_t)).astype(BF16)
                tot = _dot(sc, v_ext) + jnp.exp(inter - m_t) * _dot(qc, cst.astype(BF16))
                den = jnp.maximum(jnp.abs(tot[:, M_HEAD:M_HEAD + 1]), jnp.exp(-m_t))
                h_ref[rows, hs] = (tot[:, 0:M_HEAD] / den).astype(h_ref.dtype)

                g_col = b_tot + r_col
                m_new = jnp.maximum(b_tot + m, jnp.max(g_col, axis=0, keepdims=True))
                kw = (kc.astype(F32) * jnp.exp(g_col - m_new)).astype(BF16)
                c_scr[ch] = jnp.exp(b_tot + m - m_new) * cst + lax.dot_general(
                    kw, v_ext, _TN, preferred_element_type=F32)
                m_scr[ch] = jnp.broadcast_to(m_new, m_scr.shape[1:])


def _mscan(q, k, v, gates, bias, *, b, s, sub=4):
    t, mw = q.shape
    blk = sub * CHUNK
    nb = s // blk
    fwd = lambda w: pl.BlockSpec((blk, w), lambda bi, j: (bi * nb + j, 0))
    bwd = lambda w: pl.BlockSpec((blk, w), lambda bi, j: (bi * nb + nb - 1 - j, 0))
    out = jax.ShapeDtypeStruct((t, mw), BF16)
    return pl.pallas_call(
        functools.partial(_mscan_kernel, sub=sub),
        out_shape=(out, out),
        grid=(b, nb),
        in_specs=[fwd(mw), fwd(mw), fwd(mw), fwd(GATE_W), bwd(mw), bwd(mw), bwd(mw), bwd(GATE_W),
                  _resident((1, GATE_W))],
        out_specs=(fwd(mw), bwd(mw)),
        scratch_shapes=[pltpu.VMEM((2 * M_HEADS, M_HEAD, 2 * M_HEAD), F32),
                        pltpu.VMEM((2 * M_HEADS, 8, LANES), F32)],
        compiler_params=_params("parallel", "arbitrary"),
        name="m_scan",
    )(q, k, v, gates, q, k, v, gates, bias)


def _merge_kernel(x_ref, at_ref, hf_ref, hb_ref, xc_ref, opre_ref, ga_ref, gb_ref,
                  mnw_ref, skip_ref, wa_ref, wb_ref, wo_ref, o_ref):
    hc = hf_ref[...].astype(F32) + hb_ref[...].astype(F32)
    hn = jnp.concatenate(
        [_rms(hc[:, h * M_HEAD:(h + 1) * M_HEAD], mnw_ref[:, h * M_HEAD:(h + 1) * M_HEAD])
         for h in range(M_HEADS)], axis=1)
    bm = jax.nn.sigmoid(opre_ref[...].astype(F32)) * (hn + skip_ref[...] * xc_ref[...].astype(F32))
    ya = lax.dot_general(at_ref[0], wa_ref[...], _TN, preferred_element_type=F32)
    yb = _dot(bm.astype(BF16), wb_ref[...])
    merged = (jax.nn.sigmoid(ga_ref[...].astype(F32)) * ya + jax.nn.sigmoid(gb_ref[...].astype(F32)) * yb)
    o_ref[...] = x_ref[...] + _dot(merged.astype(BF16), wo_ref[...])


def _merge(x, at, hf, hb, xc, opre, ga, gb, mnw, skip, wa, wb, wo, *, b, s, tm=512):
    t, d = x.shape
    mw = hf.shape[1]
    nt = s // tm
    row = lambda w: pl.BlockSpec((tm, w), lambda bi, i: (bi * nt + i, 0))
    return pl.pallas_call(
        _merge_kernel,
        out_shape=jax.ShapeDtypeStruct((t, d), F32),
        grid=(b, nt),
        in_specs=[row(d), pl.BlockSpec((1, at.shape[1], tm), lambda bi, i: (bi, 0, i)),
                  row(mw), row(mw), row(mw), row(mw), row(d), row(d)]
                 + [_resident(a.shape) for a in (mnw, skip, wa, wb, wo)],
        out_specs=row(d),
        compiler_params=_params("parallel", "parallel"),
        name="merge",
    )(x, at, hf, hb, xc, opre, ga, gb, mnw, skip, wa, wb, wo)


def _pad_cols(w, width):
    return jnp.pad(w, ((0, 0), (0, width - w.shape[1])))


def _head_pad(w, head_w):
    r = w.shape[0]
    w = w.reshape(r, MLA_HEADS, head_w)
    return jnp.pad(w, ((0, 0), (0, 0), (0, LANES - head_w))).reshape(r, MLA_HEADS * LANES)


def kernel(x, positions, ffn1_norm_w, ffn1_w_gate, ffn1_w_up, ffn1_w_down, mix_norm_w, w_in, q_a_norm_w, w_uq, kv_a_norm_w, w_uk, w_uv, q_norm_w, k_norm_w, w_branch_a, conv_w, conv_b, w_mq, w_mk, w_mv, b_igate, b_fgate, m_norm_w, m_skip, w_branch_b, w_out, ffn2_norm_w, ffn2_w_gate, ffn2_w_up, ffn2_w_down, final_norm_w):
    b, s, d = x.shape
    depth = w_in.shape[0]
    q_lora, kv_lora = q_a_norm_w.shape[1], kv_a_norm_w.shape[1]
    mw = m_norm_w.shape[1]
    bf = lambda a: a.astype(BF16)
    row = lambda a: a.reshape(1, -1).astype(F32)

    half = QK_ROPE // 2
    inv = ROPE_THETA ** (-jnp.arange(half, dtype=F32) / half)
    zeros = lambda n: jnp.zeros((n,), F32)
    invl = jnp.concatenate([zeros(QK_NOPE), inv, inv, zeros(LANES - QK_HEAD)]).reshape(1, LANES)
    sgn = jnp.concatenate([zeros(QK_NOPE), -jnp.ones((half,), F32), jnp.ones((half,), F32),
                           zeros(LANES - QK_HEAD)]).reshape(1, LANES)
    q_scale = QK_HEAD ** -0.5 * math.log2(math.e)

    xt = x.reshape(b * s, d)
    pos = positions.reshape(b * s, 1)
    for l in range(depth):
        xt = _ffn(xt, row(ffn1_norm_w[l]), bf(ffn1_w_gate[l]), bf(ffn1_w_up[l]), bf(ffn1_w_down[l]),
                  row(final_norm_w[l]), final_norm=False)

        w = w_in[l]
        o_lat = q_lora + kv_lora + QK_ROPE
        o_g = o_lat + 2 * mw
        o_ga = o_g + 4 * M_HEADS
        wcat = jnp.concatenate(
            [_pad_cols(w[:, :o_lat], LAT_W), w[:, o_lat:o_lat + mw], w[:, o_lat + mw:o_g],
             _pad_cols(w[:, o_g:o_ga], GATE_W), w[:, o_ga:o_ga + d], w[:, o_ga + d:]], axis=1)
        lat, m_in, o_pre, gates, g_a, g_b = _proj(xt, row(mix_norm_w[l]), bf(wcat), mw=mw)

        qh, kh, vt = _mla_prep(
            lat, pos, row(q_a_norm_w[l]), bf(_head_pad(w_uq[l], QK_HEAD)), row(kv_a_norm_w[l]),
            bf(_head_pad(w_uk[l], QK_NOPE)), bf(w_uv[l].T), _pad_cols(row(q_norm_w[l]), LANES),
            _pad_cols(row(k_norm_w[l]), LANES), invl, sgn,
            b=b, s=s, q_lora=q_lora, kv_lora=kv_lora, q_scale=q_scale)
        at = _attn(qh, kh, vt).reshape(b, MLA_HEADS * V_HEAD, s)

        xc, mq, mk, mv = _mprep(m_in, conv_w[l], row(conv_b[l]), bf(w_mq[l]), bf(w_mk[l]), bf(w_mv[l]),
                                b=b, s=s)
        gate_bias = _pad_cols(jnp.concatenate([row(b_igate[l]), row(b_fgate[l])], axis=1), GATE_W)
        hf, hb = _mscan(mq, mk, mv, gates, gate_bias, b=b, s=s)

        xt = _merge(xt, at, hf, hb, xc, o_pre, g_a, g_b, row(m_norm_w[l]), row(m_skip[l]),
                    bf(w_branch_a[l]), bf(w_branch_b[l]), bf(w_out[l]), b=b, s=s)
        xt = _ffn(xt, row(ffn2_norm_w[l]), bf(ffn2_w_gate[l]), bf(ffn2_w_up[l]), bf(ffn2_w_down[l]),
                  row(final_norm_w[l]), final_norm=True)
    return xt.reshape(b, s, d)
```

```python
import functools
import math

import jax
import jax.numpy as jnp
from jax import lax
from jax.experimental import pallas as pl
from jax.experimental.pallas import tpu as pltpu

F32 = jnp.float32
BF16 = jnp.bfloat16

EPS = 1e-6
ROPE_THETA = 10000.0
LANES = 128
MLA_HEADS = 8
QK_NOPE = 64
QK_ROPE = 32
QK_HEAD = QK_NOPE + QK_ROPE
V_HEAD = 64
VT_ROWS = 80
M_HEADS = 4
M_HEAD = 128
CONV_K = 5
CHUNK = 128
CT_ROWS = 144
LAT_W = 512
VMEM_LIMIT = 56 * 1024 * 1024

_NT = (((1,), (1,)), ((), ()))
_TN = (((0,), (0,)), ((), ()))


def _rms(x, w):
    ms = jnp.mean(x * x, axis=-1, keepdims=True)
    return x * lax.rsqrt(ms + EPS) * w


def _dot(a, b):
    return jnp.dot(a, b, preferred_element_type=F32)


def _resident(shape):
    nd = len(shape)
    return pl.BlockSpec(shape, lambda *_: (0,) * nd, pipeline_mode=pl.Buffered(1))


def _params(*sem):
    return pltpu.CompilerParams(dimension_semantics=sem, vmem_limit_bytes=VMEM_LIMIT)


def _ffn_kernel(x_ref, nw_ref, wg_ref, wu_ref, wd_ref, fw_ref, o_ref, *, final_norm):
    x = x_ref[...]
    h = _rms(x, nw_ref[...]).astype(BF16)
    g = _dot(h, wg_ref[...])
    u = _dot(h, wu_ref[...])
    a = (g * jax.nn.sigmoid(g) * u).astype(BF16)
    out = x + 0.5 * _dot(a, wd_ref[...])
    if final_norm:
        out = _rms(out, fw_ref[...])
    o_ref[...] = out


def _ffn(x, nw, wg, wu, wd, fw, *, final_norm, tm=512):
    t, d = x.shape
    f = wg.shape[1]
    row = pl.BlockSpec((tm, d), lambda i: (i, 0))
    return pl.pallas_call(
        functools.partial(_ffn_kernel, final_norm=final_norm),
        out_shape=jax.ShapeDtypeStruct((t, d), F32),
        grid=(t // tm,),
        in_specs=[row, _resident((1, d)), _resident((d, f)), _resident((d, f)),
                  _resident((f, d)), _resident((1, d))],
        out_specs=row,
        compiler_params=_params("parallel"),
        name="ffn",
    )(x, nw, wg, wu, wd, fw)


def _proj_kernel(x_ref, nw_ref, w_ref, wgt_ref, gbias_ref, lat_ref, min_ref, opre_ref, ga_ref, gb_ref, gt_ref,
                 *, d, mw):
    h = _rms(x_ref[...], nw_ref[...]).astype(BF16)
    o = 0
    for ref, width in ((lat_ref, LAT_W), (min_ref, mw), (opre_ref, mw), (ga_ref, d), (gb_ref, d)):
        ref[...] = _dot(h, w_ref[:, o:o + width]).astype(ref.dtype)
        o += width
    gt_ref[...] = lax.dot_general(wgt_ref[...], h, _NT, preferred_element_type=F32) + gbias_ref[...]


def _proj(x, nw, wcat, wgt, gbias, *, mw, tm=512):
    t, d = x.shape
    n = wcat.shape[1]
    ng = wgt.shape[0]
    row = lambda w: pl.BlockSpec((tm, w), lambda i: (i, 0))
    out_shape = (jax.ShapeDtypeStruct((t, LAT_W), F32), jax.ShapeDtypeStruct((t, mw), F32),
                 jax.ShapeDtypeStruct((t, mw), BF16), jax.ShapeDtypeStruct((t, d), BF16),
                 jax.ShapeDtypeStruct((t, d), BF16), jax.ShapeDtypeStruct((ng, t), F32))
    return pl.pallas_call(
        functools.partial(_proj_kernel, d=d, mw=mw),
        out_shape=out_shape,
        grid=(t // tm,),
        in_specs=[row(d), _resident((1, d)), _resident((d, n)), _resident((ng, d)), _resident((ng, 1))],
        out_specs=(row(LAT_W), row(mw), row(mw), row(d), row(d), pl.BlockSpec((ng, tm), lambda i: (0, i))),
        compiler_params=_params("parallel"),
        name="proj",
    )(x, nw, wcat, wgt, gbias)


def _mla_prep_kernel(lat_ref, pos_ref, qaw_ref, wuq_ref, kvaw_ref, wuk_ref, wuvt_ref, qnw_ref, knw_ref,
                     inv_ref, q_ref, k_ref, vt_ref, *, q_lora, kv_lora, q_scale):
    tm = lat_ref.shape[0]
    lat = lat_ref[...]
    cqn = _rms(lat[:, 0:q_lora], qaw_ref[...]).astype(BF16)
    ckvn = _rms(lat[:, q_lora:q_lora + kv_lora], kvaw_ref[...]).astype(BF16)
    kpe = pltpu.roll(lat[:, q_lora + kv_lora:LAT_W], QK_NOPE, axis=1)
    qf = _dot(cqn, wuq_ref[...])
    kf = _dot(ckvn, wuk_ref[...])
    vt = lax.dot_general(wuvt_ref[...], ckvn, _NT, preferred_element_type=F32)

    ang = inv_ref[...] * pos_ref[...].astype(F32)
    cos_t, sin_t = jnp.cos(ang), jnp.sin(ang)
    fill = lambda rows, v: jnp.full((rows, tm), v, F32)
    cos = jnp.concatenate([fill(QK_NOPE, 1.0), cos_t, cos_t, fill(LANES - QK_HEAD, 1.0)], axis=0).T
    sin = jnp.concatenate([fill(QK_NOPE, 0.0), -sin_t, sin_t, fill(LANES - QK_HEAD, 0.0)], axis=0).T
    lane = lax.broadcasted_iota(jnp.int32, (tm, LANES), 1)
    first_half = lane < QK_NOPE + QK_ROPE // 2

    def norm_rope(x, w):
        ms = jnp.sum(x * x, axis=-1, keepdims=True) * (1.0 / QK_HEAD)
        y = x * lax.rsqrt(ms + EPS) * w
        partner = jnp.where(first_half, pltpu.roll(y, LANES - QK_ROPE // 2, axis=1),
                            pltpu.roll(y, QK_ROPE // 2, axis=1))
        return y * cos + partner * sin

    ones_row = jnp.where(lax.broadcasted_iota(jnp.int32, (VT_ROWS - V_HEAD, tm), 0) == 0, 1.0, 0.0).astype(BF16)
    for h in range(MLA_HEADS):
        sl = slice(h * LANES, (h + 1) * LANES)
        q_ref[0, h] = (norm_rope(qf[:, sl], qnw_ref[...]) * q_scale).astype(BF16)
        k_ref[0, h] = norm_rope(kf[:, sl] + kpe, knw_ref[...]).astype(BF16)
        vt_ref[0, h, 0:V_HEAD, :] = vt[h * V_HEAD:(h + 1) * V_HEAD, :].astype(BF16)
        vt_ref[0, h, V_HEAD:VT_ROWS, :] = ones_row


def _mla_prep(lat, pos, qaw, wuq, kvaw, wuk, wuvt, qnw, knw, inv, *, b, s, q_lora, kv_lora, q_scale, tm=512):
    nt = s // tm
    hq = MLA_HEADS
    row = lambda w: pl.BlockSpec((tm, w), lambda bi, i: (bi * nt + i, 0))
    qk_spec = pl.BlockSpec((1, hq, tm, LANES), lambda bi, i: (bi, 0, i, 0))
    in_specs = [row(LAT_W), pl.BlockSpec((1, tm), lambda bi, i: (0, bi * nt + i))] + [
        _resident(a.shape) for a in (qaw, wuq, kvaw, wuk, wuvt, qnw, knw, inv)]
    return pl.pallas_call(
        functools.partial(_mla_prep_kernel, q_lora=q_lora, kv_lora=kv_lora, q_scale=q_scale),
        out_shape=(jax.ShapeDtypeStruct((b, hq, s, LANES), BF16), jax.ShapeDtypeStruct((b, hq, s, LANES), BF16),
                   jax.ShapeDtypeStruct((b, hq, VT_ROWS, s), BF16)),
        grid=(b, nt),
        in_specs=in_specs,
        out_specs=(qk_spec, qk_spec, pl.BlockSpec((1, hq, VT_ROWS, tm), lambda bi, i: (bi, 0, 0, i))),
        compiler_params=_params("parallel", "parallel"),
        name="mla_prep",
    )(lat, pos, qaw, wuq, kvaw, wuk, wuvt, qnw, knw, inv)


def _attn_kernel(q_ref, k_ref, vt_ref, o_ref, s0_scr, s1_scr, *, tk):
    tq = q_ref.shape[2]
    nk = k_ref.shape[2] // tk
    q = q_ref[0, 0]

    def chunk(c):
        return pl.ds(pl.multiple_of(c * tk, tk), tk)

    def scores(c, s_scr):
        st = lax.dot_general(k_ref[0, 0, chunk(c), :], q, _NT, preferred_element_type=F32)
        s_scr[...] = st
        return jnp.max(st, axis=0, keepdims=True)

    def values(c, s_scr, m, cm, acc):
        m_new = jnp.maximum(m, cm)
        p = jnp.exp2(s_scr[...] - m_new).astype(BF16)
        return m_new, acc * jnp.exp2(m - m_new) + _dot(vt_ref[0, 0, :, chunk(c)], p)

    def pair(i, carry):
        m, cm, acc = carry
        cm1 = scores(2 * i + 1, s1_scr)
        m, acc = values(2 * i, s0_scr, m, cm, acc)
        cm0 = scores(2 * i + 2, s0_scr)
        m, acc = values(2 * i + 1, s1_scr, m, cm1, acc)
        return m, cm0, acc

    init = (jnp.full((1, tq), -jnp.inf, F32), scores(0, s0_scr), jnp.zeros((VT_ROWS, tq), F32))
    m, cm, acc = lax.fori_loop(0, nk // 2 - 1, pair, init)
    cm1 = scores(nk - 1, s1_scr)
    m, acc = values(nk - 2, s0_scr, m, cm, acc)
    m, acc = values(nk - 1, s1_scr, m, cm1, acc)
    o_ref[0, 0] = (acc[0:V_HEAD] / acc[V_HEAD:V_HEAD + 1]).astype(o_ref.dtype)


def _attn(q, k, vt, *, tq=512, tk=1024):
    b, h, s, _ = q.shape
    assert s % (2 * tk) == 0 and s % tq == 0
    return pl.pallas_call(
        functools.partial(_attn_kernel, tk=tk),
        out_shape=jax.ShapeDtypeStruct((b, h, V_HEAD, s), BF16),
        grid=(b, h, s // tq),
        in_specs=[pl.BlockSpec((1, 1, tq, LANES), lambda bi, hi, i: (bi, hi, i, 0)),
                  pl.BlockSpec((1, 1, s, LANES), lambda bi, hi, i: (bi, hi, 0, 0)),
                  pl.BlockSpec((1, 1, VT_ROWS, s), lambda bi, hi, i: (bi, hi, 0, 0))],
        out_specs=pl.BlockSpec((1, 1, V_HEAD, tq), lambda bi, hi, i: (bi, hi, 0, i)),
        scratch_shapes=[pltpu.VMEM((tk, tq), F32), pltpu.VMEM((tk, tq), F32)],
        compiler_params=_params("parallel", "parallel", "arbitrary"),
        name="attn",
    )(q, k, vt)


HALO = 8


def _mprep_kernel(cur_ref, prev_ref, next_ref, cw_ref, cb_ref, wq_ref, wk_ref, wvt_ref,
                  xc_ref, q_ref, k_ref, vt_ref, xs_ref):
    tm = cur_ref.shape[0]
    i = pl.program_id(1)
    cur = cur_ref[...]
    xs_ref[0:HALO, :] = jnp.where(i > 0, prev_ref[...], 0.0)
    xs_ref[HALO:HALO + tm, :] = cur
    xs_ref[HALO + tm:2 * HALO + tm, :] = jnp.where(i < pl.num_programs(1) - 1, next_ref[...], 0.0)
    y = cb_ref[...]
    for kk in range(CONV_K):
        y = y + cw_ref[kk:kk + 1, :] * xs_ref[pl.ds(HALO - CONV_K // 2 + kk, tm), :]
    xc = (y * jax.nn.sigmoid(y)).astype(BF16)
    xc_ref[...] = xc
    xm = cur.astype(BF16)
    for h in range(M_HEADS):
        sl = slice(h * M_HEAD, (h + 1) * M_HEAD)
        q_ref[:, sl] = _dot(xc[:, sl], wq_ref[h]).astype(BF16)
        k_ref[:, sl] = (_dot(xc[:, sl], wk_ref[h]) * (M_HEAD ** -0.5)).astype(BF16)
        vt_ref[sl, :] = lax.dot_general(wvt_ref[h], xm[:, sl], _NT, preferred_element_type=F32).astype(BF16)


def _mprep(m_in, cw, cb, wq, wk, wvt, *, b, s, tm=512):
    t, mw = m_in.shape
    nt = s // tm
    hb = tm // HALO
    row = pl.BlockSpec((tm, mw), lambda bi, i: (bi * nt + i, 0))
    prev = pl.BlockSpec((HALO, mw), lambda bi, i: (jnp.maximum((bi * nt + i) * hb - 1, 0), 0))
    nxt = pl.BlockSpec((HALO, mw), lambda bi, i: (jnp.minimum((bi * nt + i + 1) * hb, t // HALO - 1), 0))
    out = jax.ShapeDtypeStruct((t, mw), BF16)
    return pl.pallas_call(
        _mprep_kernel,
        out_shape=(out, out, out, jax.ShapeDtypeStruct((mw, t), BF16)),
        grid=(b, nt),
        in_specs=[row, prev, nxt] + [_resident(a.shape) for a in (cw, cb, wq, wk, wvt)],
        out_specs=(row, row, row, pl.BlockSpec((mw, tm), lambda bi, i: (0, bi * nt + i))),
        scratch_shapes=[pltpu.VMEM((tm + 2 * HALO, mw), F32)],
        compiler_params=_params("parallel", "parallel"),
        name="m_prep",
    )(m_in, m_in, m_in, cw, cb, wq, wk, wvt)


def _log_sigmoid(x):
    return jnp.minimum(x, 0.0) - jnp.log1p(jnp.exp(-jnp.abs(x)))


def _cumsum_lanes(x, tri):
    hi = x.astype(BF16)
    r1 = x - hi.astype(F32)
    mid = r1.astype(BF16)
    lo = (r1 - mid.astype(F32)).astype(BF16)
    return _dot(hi, tri) + _dot(mid, tri) + _dot(lo, tri)


def _mscan_kernel(qf_ref, kf_ref, vf_ref, gf_ref, qb_ref, kb_ref, vb_ref, gb_ref,
                  hf_ref, hb_ref, c_scr, m_scr, *, sub):
    L = CHUNK
    ng = 2 * M_HEADS

    @pl.when(pl.program_id(1) == 0)
    def _():
        c_scr[...] = jnp.zeros_like(c_scr)
        m_scr[...] = jnp.zeros_like(m_scr)

    row = lax.broadcasted_iota(jnp.int32, (L, L), 0)
    col = lax.broadcasted_iota(jnp.int32, (L, L), 1)
    visible = (row <= col, row >= col)
    tri = tuple(jnp.where(v, 1.0, 0.0).astype(BF16) for v in visible)
    ones_rows = jnp.where(lax.broadcasted_iota(jnp.int32, (CT_ROWS - M_HEAD, L), 0) == 0, 1.0, 0.0).astype(BF16)
    total = (L - 1, 0)
    dirs = ((qf_ref, kf_ref, vf_ref, gf_ref, hf_ref), (qb_ref, kb_ref, vb_ref, gb_ref, hb_ref))

    for c in range(sub):
        for d, (q_ref, k_ref, v_ref, g_ref, h_ref) in enumerate(dirs):
            lo = (c if d == 0 else sub - 1 - c) * L
            tok = slice(lo, lo + L)
            gt = g_ref[:, tok]
            bt = _cumsum_lanes(_log_sigmoid(gt), tri[d])
            r8 = gt[0:ng] - bt[ng:2 * ng]
            rc = jnp.concatenate([r8, jnp.zeros((L - ng, L), F32)], axis=0).T
            for h in range(M_HEADS):
                ch = d * M_HEADS + h
                hs = slice(h * M_HEAD, (h + 1) * M_HEAD)
                qc, kc = q_ref[tok, hs], k_ref[tok, hs]
                v_ext = jnp.concatenate([v_ref[hs, tok], ones_rows], axis=0)
                b_row = bt[ng + ch:ng + ch + 1, :]
                r_row = r8[ch:ch + 1, :]
                b_tot = b_row[:, total[d]:total[d] + 1]
                m = m_scr[ch, 0:1, 0:1]
                cst = c_scr[ch]

                logw = jnp.where(visible[d], b_row + rc[:, ch:ch + 1], -jnp.inf)
                mx = jnp.max(logw, axis=0, keepdims=True)
                qk = lax.dot_general(kc, qc, _NT, preferred_element_type=F32)
                intra = _dot(v_ext, (qk * jnp.exp(logw - mx)).astype(BF16))
                g_row = b_tot + r_row
                g_max = jnp.max(g_row, axis=1, keepdims=True)
                incr = _dot((v_ext.astype(F32) * jnp.exp(g_row - g_max)).astype(BF16), kc)

                inter = b_row + m
                m_t = jnp.maximum(mx, inter)
                tot = jnp.exp(mx - m_t) * intra + jnp.exp(inter - m_t) * lax.dot_general(
                    cst.astype(BF16), qc, _NT, preferred_element_type=F32)
                den = jnp.maximum(jnp.abs(tot[M_HEAD:M_HEAD + 1]), jnp.exp(-m_t))
                h_ref[tok, hs] = (tot[0:M_HEAD] * (1.0 / den)).T.astype(h_ref.dtype)
                m_new = jnp.maximum(b_tot + m, g_max)
                c_scr[ch] = jnp.exp(b_tot + m - m_new) * cst + jnp.exp(g_max - m_new) * incr
                m_scr[ch] = jnp.broadcast_to(m_new, m_scr.shape[1:])


def _mscan(q, k, vt, gt, *, b, s, sub=4):
    t, mw = q.shape
    ng = gt.shape[0]
    blk = sub * CHUNK
    nb = s // blk
    fwd = pl.BlockSpec((blk, mw), lambda bi, j: (bi * nb + j, 0))
    bwd = pl.BlockSpec((blk, mw), lambda bi, j: (bi * nb + nb - 1 - j, 0))
    fwd_t = lambda r: pl.BlockSpec((r, blk), lambda bi, j: (0, bi * nb + j))
    bwd_t = lambda r: pl.BlockSpec((r, blk), lambda bi, j: (0, bi * nb + nb - 1 - j))
    out = jax.ShapeDtypeStruct((t, mw), BF16)
    return pl.pallas_call(
        functools.partial(_mscan_kernel, sub=sub),
        out_shape=(out, out),
        grid=(b, nb),
        in_specs=[fwd, fwd, fwd_t(mw), fwd_t(ng), bwd, bwd, bwd_t(mw), bwd_t(ng)],
        out_specs=(fwd, bwd),
        scratch_shapes=[pltpu.VMEM((2 * M_HEADS, CT_ROWS, M_HEAD), F32),
                        pltpu.VMEM((2 * M_HEADS, 8, LANES), F32)],
        compiler_params=_params("parallel", "arbitrary"),
        name="m_scan",
    )(q, k, vt, gt, q, k, vt, gt)


def _merge_kernel(x_ref, at_ref, hf_ref, hb_ref, xc_ref, opre_ref, ga_ref, gb_ref,
                  mnw_ref, skip_ref, wa_ref, wb_ref, wo_ref, o_ref):
    hc = hf_ref[...].astype(F32) + hb_ref[...].astype(F32)
    hn = jnp.concatenate(
        [_rms(hc[:, h * M_HEAD:(h + 1) * M_HEAD], mnw_ref[:, h * M_HEAD:(h + 1) * M_HEAD])
         for h in range(M_HEADS)], axis=1)
    bm = jax.nn.sigmoid(opre_ref[...].astype(F32)) * (hn + skip_ref[...] * xc_ref[...].astype(F32))
    ya = lax.dot_general(at_ref[0], wa_ref[...], _TN, preferred_element_type=F32)
    yb = _dot(bm.astype(BF16), wb_ref[...])
    merged = (jax.nn.sigmoid(ga_ref[...].astype(F32)) * ya + jax.nn.sigmoid(gb_ref[...].astype(F32)) * yb)
    o_ref[...] = x_ref[...] + _dot(merged.astype(BF16), wo_ref[...])


def _merge(x, at, hf, hb, xc, opre, ga, gb, mnw, skip, wa, wb, wo, *, b, s, tm=512):
    t, d = x.shape
    mw = hf.shape[1]
    nt = s // tm
    row = lambda w: pl.BlockSpec((tm, w), lambda bi, i: (bi * nt + i, 0))
    return pl.pallas_call(
        _merge_kernel,
        out_shape=jax.ShapeDtypeStruct((t, d), F32),
        grid=(b, nt),
        in_specs=[row(d), pl.BlockSpec((1, at.shape[1], tm), lambda bi, i: (bi, 0, i)),
                  row(mw), row(mw), row(mw), row(mw), row(d), row(d)]
                 + [_resident(a.shape) for a in (mnw, skip, wa, wb, wo)],
        out_specs=row(d),
        compiler_params=_params("parallel", "parallel"),
        name="merge",
    )(x, at, hf, hb, xc, opre, ga, gb, mnw, skip, wa, wb, wo)


def _pad_cols(w, width):
    return jnp.pad(w, ((0, 0), (0, width - w.shape[1])))


def _head_pad(w, head_w):
    r = w.shape[0]
    w = w.reshape(r, MLA_HEADS, head_w)
    return jnp.pad(w, ((0, 0), (0, 0), (0, LANES - head_w))).reshape(r, MLA_HEADS * LANES)


def kernel(x, positions, ffn1_norm_w, ffn1_w_gate, ffn1_w_up, ffn1_w_down, mix_norm_w, w_in, q_a_norm_w, w_uq, kv_a_norm_w, w_uk, w_uv, q_norm_w, k_norm_w, w_branch_a, conv_w, conv_b, w_mq, w_mk, w_mv, b_igate, b_fgate, m_norm_w, m_skip, w_branch_b, w_out, ffn2_norm_w, ffn2_w_gate, ffn2_w_up, ffn2_w_down, final_norm_w):
    b, s, d = x.shape
    depth = w_in.shape[0]
    q_lora, kv_lora = q_a_norm_w.shape[1], kv_a_norm_w.shape[1]
    mw = m_norm_w.shape[1]
    bf = lambda a: a.astype(BF16)
    row = lambda a: a.reshape(1, -1).astype(F32)

    half = QK_ROPE // 2
    inv = (ROPE_THETA ** (-jnp.arange(half, dtype=F32) / half)).reshape(half, 1)
    q_scale = QK_HEAD ** -0.5 * math.log2(math.e)

    xt = x.reshape(b * s, d)
    pos = positions.reshape(1, b * s)
    for l in range(depth):
        xt = _ffn(xt, row(ffn1_norm_w[l]), bf(ffn1_w_gate[l]), bf(ffn1_w_up[l]), bf(ffn1_w_down[l]),
                  row(final_norm_w[l]), final_norm=False)

        w = w_in[l]
        o_lat = q_lora + kv_lora + QK_ROPE
        o_g = o_lat + 2 * mw
        o_ga = o_g + 4 * M_HEADS
        wcat = jnp.concatenate(
            [_pad_cols(w[:, :o_lat], LAT_W), w[:, o_lat:o_lat + mw], w[:, o_lat + mw:o_g],
             w[:, o_ga:o_ga + d], w[:, o_ga + d:]], axis=1)
        gbias = jnp.concatenate([b_igate[l], b_fgate[l]]).reshape(-1, 1).astype(F32)
        lat, m_in, o_pre, g_a, g_b, gt = _proj(xt, row(mix_norm_w[l]), bf(wcat), bf(w[:, o_g:o_ga].T), gbias,
                                               mw=mw)

        qh, kh, vt = _mla_prep(
            lat, pos, row(q_a_norm_w[l]), bf(_head_pad(w_uq[l], QK_HEAD)), row(kv_a_norm_w[l]),
            bf(_head_pad(w_uk[l], QK_NOPE)), bf(w_uv[l].T), _pad_cols(row(q_norm_w[l]), LANES),
            _pad_cols(row(k_norm_w[l]), LANES), inv,
            b=b, s=s, q_lora=q_lora, kv_lora=kv_lora, q_scale=q_scale)
        at = _attn(qh, kh, vt).reshape(b, MLA_HEADS * V_HEAD, s)

        xc, mq, mk, mvt = _mprep(m_in, conv_w[l], row(conv_b[l]), bf(w_mq[l]), bf(w_mk[l]),
                                 bf(jnp.swapaxes(w_mv[l], 1, 2)), b=b, s=s)
        hf, hb = _mscan(mq, mk, mvt, gt, b=b, s=s)

        xt = _merge(xt, at, hf, hb, xc, o_pre, g_a, g_b, row(m_norm_w[l]), row(m_skip[l]),
                    bf(w_branch_a[l]), bf(w_branch_b[l]), bf(w_out[l]), b=b, s=s)
        xt = _ffn(xt, row(ffn2_norm_w[l]), bf(ffn2_w_gate[l]), bf(ffn2_w_up[l]), bf(ffn2_w_down[l]),
                  row(final_norm_w[l]), final_norm=True)
    return xt.reshape(b, s, d)
```

```python
import functools
import math

import jax
import jax.numpy as jnp
from jax import lax
from jax.experimental import pallas as pl
from jax.experimental.pallas import tpu as pltpu

F32 = jnp.float32
BF16 = jnp.bfloat16

EPS = 1e-6
ROPE_THETA = 10000.0
LANES = 128
MLA_HEADS = 8
QK_NOPE = 64
QK_ROPE = 32
QK_HEAD = QK_NOPE + QK_ROPE
V_HEAD = 64
VT_ROWS = 80
M_HEADS = 4
M_HEAD = 128
CONV_K = 5
CHUNK = 256
CT_ROWS = 144
LAT_W = 512
VMEM_LIMIT = 56 * 1024 * 1024

_NT = (((1,), (1,)), ((), ()))
_TN = (((0,), (0,)), ((), ()))


def _rms(x, w):
    ms = jnp.mean(x * x, axis=-1, keepdims=True)
    return x * lax.rsqrt(ms + EPS) * w


def _dot(a, b):
    return jnp.dot(a, b, preferred_element_type=F32)


def _resident(shape):
    nd = len(shape)
    return pl.BlockSpec(shape, lambda *_: (0,) * nd, pipeline_mode=pl.Buffered(1))


def _params(*sem):
    return pltpu.CompilerParams(dimension_semantics=sem, vmem_limit_bytes=VMEM_LIMIT)


def _ffn_kernel(x_ref, nw_ref, wg_ref, wu_ref, wd_ref, fw_ref, o_ref, *, final_norm):
    x = x_ref[...]
    h = _rms(x, nw_ref[...]).astype(BF16)
    g = _dot(h, wg_ref[...])
    u = _dot(h, wu_ref[...])
    a = (g * jax.nn.sigmoid(g) * u).astype(BF16)
    out = x + 0.5 * _dot(a, wd_ref[...])
    if final_norm:
        out = _rms(out, fw_ref[...])
    o_ref[...] = out


def _ffn(x, nw, wg, wu, wd, fw, *, final_norm, tm=512):
    t, d = x.shape
    f = wg.shape[1]
    row = pl.BlockSpec((tm, d), lambda i: (i, 0))
    return pl.pallas_call(
        functools.partial(_ffn_kernel, final_norm=final_norm),
        out_shape=jax.ShapeDtypeStruct((t, d), F32),
        grid=(t // tm,),
        in_specs=[row, _resident((1, d)), _resident((d, f)), _resident((d, f)),
                  _resident((f, d)), _resident((1, d))],
        out_specs=row,
        compiler_params=_params("parallel"),
        name="ffn",
    )(x, nw, wg, wu, wd, fw)


def _proj_kernel(x_ref, nw_ref, w_ref, wgt_ref, gbias_ref, lat_ref, min_ref, opre_ref, ga_ref, gb_ref, gt_ref,
                 *, d, mw):
    h = _rms(x_ref[...], nw_ref[...]).astype(BF16)
    o = 0
    for ref, width in ((lat_ref, LAT_W), (min_ref, mw), (opre_ref, mw), (ga_ref, d), (gb_ref, d)):
        ref[...] = _dot(h, w_ref[:, o:o + width]).astype(ref.dtype)
        o += width
    gt_ref[...] = lax.dot_general(wgt_ref[...], h, _NT, preferred_element_type=F32) + gbias_ref[...]


def _proj(x, nw, wcat, wgt, gbias, *, mw, tm=512):
    t, d = x.shape
    n = wcat.shape[1]
    ng = wgt.shape[0]
    row = lambda w: pl.BlockSpec((tm, w), lambda i: (i, 0))
    out_shape = (jax.ShapeDtypeStruct((t, LAT_W), F32), jax.ShapeDtypeStruct((t, mw), F32),
                 jax.ShapeDtypeStruct((t, mw), BF16), jax.ShapeDtypeStruct((t, d), BF16),
                 jax.ShapeDtypeStruct((t, d), BF16), jax.ShapeDtypeStruct((ng, t), F32))
    return pl.pallas_call(
        functools.partial(_proj_kernel, d=d, mw=mw),
        out_shape=out_shape,
        grid=(t // tm,),
        in_specs=[row(d), _resident((1, d)), _resident((d, n)), _resident((ng, d)), _resident((ng, 1))],
        out_specs=(row(LAT_W), row(mw), row(mw), row(d), row(d), pl.BlockSpec((ng, tm), lambda i: (0, i))),
        compiler_params=_params("parallel"),
        name="proj",
    )(x, nw, wcat, wgt, gbias)


def _mla_prep_kernel(lat_ref, pos_ref, qaw_ref, wuqt_ref, kvaw_ref, wuk_ref, wuvt_ref, qnw_ref, knw_ref,
                     inv_ref, qt_ref, k_ref, vt_ref, *, q_lora, kv_lora, q_scale):
    tm = lat_ref.shape[0]
    lat = lat_ref[...]
    cqn = _rms(lat[:, 0:q_lora], qaw_ref[...]).astype(BF16)
    ckvn = _rms(lat[:, q_lora:q_lora + kv_lora], kvaw_ref[...]).astype(BF16)
    kpe = pltpu.roll(lat[:, q_lora + kv_lora:LAT_W], QK_NOPE, axis=1)
    qft = lax.dot_general(wuqt_ref[...], cqn, _NT, preferred_element_type=F32)
    kf = _dot(ckvn, wuk_ref[...])
    vt = lax.dot_general(wuvt_ref[...], ckvn, _NT, preferred_element_type=F32)

    ang = inv_ref[...] * pos_ref[...].astype(F32)
    cos_t, sin_t = jnp.cos(ang), jnp.sin(ang)
    fill = lambda rows, v: jnp.full((rows, tm), v, F32)
    cos = jnp.concatenate([fill(QK_NOPE, 1.0), cos_t, cos_t, fill(LANES - QK_HEAD, 1.0)], axis=0).T
    sin = jnp.concatenate([fill(QK_NOPE, 0.0), -sin_t, sin_t, fill(LANES - QK_HEAD, 0.0)], axis=0).T
    lane = lax.broadcasted_iota(jnp.int32, (tm, LANES), 1)
    first_half = lane < QK_NOPE + QK_ROPE // 2

    def norm_rope(x, w):
        ms = jnp.sum(x * x, axis=-1, keepdims=True) * (1.0 / QK_HEAD)
        y = x * lax.rsqrt(ms + EPS) * w
        partner = jnp.where(first_half, pltpu.roll(y, LANES - QK_ROPE // 2, axis=1),
                            pltpu.roll(y, QK_ROPE // 2, axis=1))
        return y * cos + partner * sin

    def norm_rope_t(xt, w):
        ms = jnp.sum(xt * xt, axis=0, keepdims=True) * (1.0 / QK_HEAD)
        y = xt * lax.rsqrt(ms + EPS) * w
        y1, y2 = y[QK_NOPE:QK_NOPE + QK_ROPE // 2], y[QK_NOPE + QK_ROPE // 2:QK_HEAD]
        return jnp.concatenate([y[0:QK_NOPE], y1 * cos_t - y2 * sin_t, y2 * cos_t + y1 * sin_t, y[QK_HEAD:]], axis=0)

    ones_row = jnp.where(lax.broadcasted_iota(jnp.int32, (VT_ROWS - V_HEAD, tm), 0) == 0, 1.0, 0.0).astype(BF16)
    for h in range(MLA_HEADS):
        sl = slice(h * LANES, (h + 1) * LANES)
        qt_ref[0, h] = (norm_rope_t(qft[sl, :], qnw_ref[...]) * q_scale).astype(BF16)
        k_ref[0, h] = norm_rope(kf[:, sl] + kpe, knw_ref[...]).astype(BF16)
        vt_ref[0, h, 0:V_HEAD, :] = vt[h * V_HEAD:(h + 1) * V_HEAD, :].astype(BF16)
        vt_ref[0, h, V_HEAD:VT_ROWS, :] = ones_row


def _mla_prep(lat, pos, qaw, wuqt, kvaw, wuk, wuvt, qnw, knw, inv, *, b, s, q_lora, kv_lora, q_scale, tm=512):
    nt = s // tm
    hq = MLA_HEADS
    row = lambda w: pl.BlockSpec((tm, w), lambda bi, i: (bi * nt + i, 0))
    qk_spec = pl.BlockSpec((1, hq, tm, LANES), lambda bi, i: (bi, 0, i, 0))
    in_specs = [row(LAT_W), pl.BlockSpec((1, tm), lambda bi, i: (0, bi * nt + i))] + [
        _resident(a.shape) for a in (qaw, wuqt, kvaw, wuk, wuvt, qnw, knw, inv)]
    return pl.pallas_call(
        functools.partial(_mla_prep_kernel, q_lora=q_lora, kv_lora=kv_lora, q_scale=q_scale),
        out_shape=(jax.ShapeDtypeStruct((b, hq, LANES, s), BF16), jax.ShapeDtypeStruct((b, hq, s, LANES), BF16),
                   jax.ShapeDtypeStruct((b, hq, VT_ROWS, s), BF16)),
        grid=(b, nt),
        in_specs=in_specs,
        out_specs=(pl.BlockSpec((1, hq, LANES, tm), lambda bi, i: (bi, 0, 0, i)), qk_spec,
                   pl.BlockSpec((1, hq, VT_ROWS, tm), lambda bi, i: (bi, 0, 0, i))),
        compiler_params=_params("parallel", "parallel"),
        name="mla_prep",
    )(lat, pos, qaw, wuqt, kvaw, wuk, wuvt, qnw, knw, inv)


def _attn_kernel(qt_ref, k_ref, vt_ref, o_ref, s0_scr, s1_scr, *, tk):
    tq = qt_ref.shape[3]
    nk = k_ref.shape[2] // tk
    qt = qt_ref[0, 0]

    def chunk(c):
        return pl.ds(pl.multiple_of(c * tk, tk), tk)

    def scores(c, s_scr):
        st = _dot(k_ref[0, 0, chunk(c), :], qt)
        s_scr[...] = st
        return jnp.max(st, axis=0, keepdims=True)

    def values(c, s_scr, m, cm, acc):
        m_new = jnp.maximum(m, cm)
        p = jnp.exp2(s_scr[...] - m_new).astype(BF16)
        return m_new, acc * jnp.exp2(m - m_new) + _dot(vt_ref[0, 0, :, chunk(c)], p)

    def pair(i, carry):
        m, cm, acc = carry
        cm1 = scores(2 * i + 1, s1_scr)
        m, acc = values(2 * i, s0_scr, m, cm, acc)
        cm0 = scores(2 * i + 2, s0_scr)
        m, acc = values(2 * i + 1, s1_scr, m, cm1, acc)
        return m, cm0, acc

    init = (jnp.full((1, tq), -jnp.inf, F32), scores(0, s0_scr), jnp.zeros((VT_ROWS, tq), F32))
    m, cm, acc = lax.fori_loop(0, nk // 2 - 1, pair, init)
    cm1 = scores(nk - 1, s1_scr)
    m, acc = values(nk - 2, s0_scr, m, cm, acc)
    m, acc = values(nk - 1, s1_scr, m, cm1, acc)
    o_ref[0, 0] = (acc[0:V_HEAD] / acc[V_HEAD:V_HEAD + 1]).astype(o_ref.dtype)


def _attn(qt, k, vt, *, tq=512, tk=1024):
    b, h, s, _ = k.shape
    assert s % (2 * tk) == 0 and s % tq == 0
    return pl.pallas_call(
        functools.partial(_attn_kernel, tk=tk),
        out_shape=jax.ShapeDtypeStruct((b, h, V_HEAD, s), BF16),
        grid=(b, h, s // tq),
        in_specs=[pl.BlockSpec((1, 1, LANES, tq), lambda bi, hi, i: (bi, hi, 0, i)),
                  pl.BlockSpec((1, 1, s, LANES), lambda bi, hi, i: (bi, hi, 0, 0)),
                  pl.BlockSpec((1, 1, VT_ROWS, s), lambda bi, hi, i: (bi, hi, 0, 0))],
        out_specs=pl.BlockSpec((1, 1, V_HEAD, tq), lambda bi, hi, i: (bi, hi, 0, i)),
        scratch_shapes=[pltpu.VMEM((tk, tq), F32), pltpu.VMEM((tk, tq), F32)],
        compiler_params=_params("parallel", "parallel", "arbitrary"),
        name="attn",
    )(qt, k, vt)


HALO = 8


def _mprep_kernel(cur_ref, prev_ref, next_ref, cw_ref, cb_ref, wqt_ref, wk_ref, wvt_ref,
                  xc_ref, qt_ref, k_ref, vt_ref, xs_ref):
    tm = cur_ref.shape[0]
    i = pl.program_id(1)
    cur = cur_ref[...]
    xs_ref[0:HALO, :] = jnp.where(i > 0, prev_ref[...], 0.0)
    xs_ref[HALO:HALO + tm, :] = cur
    xs_ref[HALO + tm:2 * HALO + tm, :] = jnp.where(i < pl.num_programs(1) - 1, next_ref[...], 0.0)
    y = cb_ref[...]
    for kk in range(CONV_K):
        y = y + cw_ref[kk:kk + 1, :] * xs_ref[pl.ds(HALO - CONV_K // 2 + kk, tm), :]
    xc = (y * jax.nn.sigmoid(y)).astype(BF16)
    xc_ref[...] = xc
    xm = cur.astype(BF16)
    for h in range(M_HEADS):
        sl = slice(h * M_HEAD, (h + 1) * M_HEAD)
        qt_ref[sl, :] = lax.dot_general(wqt_ref[h], xc[:, sl], _NT, preferred_element_type=F32).astype(BF16)
        k_ref[:, sl] = (_dot(xc[:, sl], wk_ref[h]) * (M_HEAD ** -0.5)).astype(BF16)
        vt_ref[sl, :] = lax.dot_general(wvt_ref[h], xm[:, sl], _NT, preferred_element_type=F32).astype(BF16)


def _mprep(m_in, cw, cb, wqt, wk, wvt, *, b, s, tm=512):
    t, mw = m_in.shape
    nt = s // tm
    hb = tm // HALO
    row = pl.BlockSpec((tm, mw), lambda bi, i: (bi * nt + i, 0))
    prev = pl.BlockSpec((HALO, mw), lambda bi, i: (jnp.maximum((bi * nt + i) * hb - 1, 0), 0))
    nxt = pl.BlockSpec((HALO, mw), lambda bi, i: (jnp.minimum((bi * nt + i + 1) * hb, t // HALO - 1), 0))
    col = pl.BlockSpec((mw, tm), lambda bi, i: (0, bi * nt + i))
    out = jax.ShapeDtypeStruct((t, mw), BF16)
    out_t = jax.ShapeDtypeStruct((mw, t), BF16)
    return pl.pallas_call(
        _mprep_kernel,
        out_shape=(out, out_t, out, out_t),
        grid=(b, nt),
        in_specs=[row, prev, nxt] + [_resident(a.shape) for a in (cw, cb, wqt, wk, wvt)],
        out_specs=(row, col, row, col),
        scratch_shapes=[pltpu.VMEM((tm + 2 * HALO, mw), F32)],
        compiler_params=_params("parallel", "parallel"),
        name="m_prep",
    )(m_in, m_in, m_in, cw, cb, wqt, wk, wvt)


def _log_sigmoid(x):
    return jnp.minimum(x, 0.0) - jnp.log1p(jnp.exp(-jnp.abs(x)))


def _cumsum_lanes(x, tri):
    n = x.shape[0]
    hi = x.astype(BF16)
    r1 = x - hi.astype(F32)
    mid = r1.astype(BF16)
    lo = (r1 - mid.astype(F32)).astype(BF16)
    y = _dot(jnp.concatenate([hi, mid, lo], axis=0), tri)
    return y[0:n] + y[n:2 * n] + y[2 * n:3 * n]


def _mscan_kernel(qf_ref, kf_ref, vf_ref, gf_ref, qb_ref, kb_ref, vb_ref, gb_ref,
                  hf_ref, hb_ref, c_scr, m_scr, *, sub):
    L = CHUNK
    ng = 2 * M_HEADS

    @pl.when(pl.program_id(1) == 0)
    def _():
        c_scr[...] = jnp.zeros_like(c_scr)
        m_scr[...] = jnp.zeros_like(m_scr)

    row = lax.broadcasted_iota(jnp.int32, (L, L), 0)
    col = lax.broadcasted_iota(jnp.int32, (L, L), 1)
    visible = (row <= col, row >= col)
    tri = tuple(jnp.where(v, 1.0, 0.0).astype(BF16) for v in visible)
    ones_rows = jnp.where(lax.broadcasted_iota(jnp.int32, (CT_ROWS - M_HEAD, L), 0) == 0, 1.0, 0.0).astype(BF16)
    total = (L - 1, 0)
    dirs = ((qf_ref, kf_ref, vf_ref, gf_ref, hf_ref), (qb_ref, kb_ref, vb_ref, gb_ref, hb_ref))

    for c in range(sub):
        for d, (q_ref, k_ref, v_ref, g_ref, h_ref) in enumerate(dirs):
            lo = (c if d == 0 else sub - 1 - c) * L
            tok = slice(lo, lo + L)
            gt = g_ref[:, tok]
            bt = _cumsum_lanes(_log_sigmoid(gt), tri[d])
            r8 = gt[0:ng] - bt[ng:2 * ng]
            rc = jnp.concatenate([r8, jnp.zeros((LANES - ng, L), F32)], axis=0).T
            for h in range(M_HEADS):
                ch = d * M_HEADS + h
                hs = slice(h * M_HEAD, (h + 1) * M_HEAD)
                qt, kc = q_ref[hs, tok], k_ref[tok, hs]
                v_ext = jnp.concatenate([v_ref[hs, tok], ones_rows], axis=0)
                b_row = bt[ng + ch:ng + ch + 1, :]
                r_row = r8[ch:ch + 1, :]
                b_tot = b_row[:, total[d]:total[d] + 1]
                m = m_scr[ch, 0:1, 0:1]
                cst = c_scr[ch]

                logw = jnp.where(visible[d], b_row + rc[:, ch:ch + 1], -jnp.inf)
                mx = jnp.max(logw, axis=0, keepdims=True)
                kq = _dot(jnp.concatenate([kc, cst.astype(BF16)], axis=0), qt)
                intra = _dot(v_ext, (kq[0:L] * jnp.exp(logw - mx)).astype(BF16))
                g_row = b_tot + r_row
                g_max = jnp.max(g_row, axis=1, keepdims=True)
                incr = _dot((v_ext.astype(F32) * jnp.exp(g_row - g_max)).astype(BF16), kc)

                inter = b_row + m
                m_t = jnp.maximum(mx, inter)
                tot = jnp.exp(mx - m_t) * intra + jnp.exp(inter - m_t) * kq[L:L + CT_ROWS]
                den = jnp.maximum(jnp.abs(tot[M_HEAD:M_HEAD + 1]), jnp.exp(-m_t))
                h_ref[tok, hs] = (tot[0:M_HEAD] * (1.0 / den)).T.astype(h_ref.dtype)
                m_new = jnp.maximum(b_tot + m, g_max)
                c_scr[ch] = jnp.exp(b_tot + m - m_new) * cst + jnp.exp(g_max - m_new) * incr
                m_scr[ch] = jnp.broadcast_to(m_new, m_scr.shape[1:])


def _mscan(qt, k, vt, gt, *, b, s, sub=2):
    t, mw = k.shape
    ng = gt.shape[0]
    blk = sub * CHUNK
    nb = s // blk
    fwd = pl.BlockSpec((blk, mw), lambda bi, j: (bi * nb + j, 0))
    bwd = pl.BlockSpec((blk, mw), lambda bi, j: (bi * nb + nb - 1 - j, 0))
    fwd_t = lambda r: pl.BlockSpec((r, blk), lambda bi, j: (0, bi * nb + j))
    bwd_t = lambda r: pl.BlockSpec((r, blk), lambda bi, j: (0, bi * nb + nb - 1 - j))
    out = jax.ShapeDtypeStruct((t, mw), BF16)
    return pl.pallas_call(
        functools.partial(_mscan_kernel, sub=sub),
        out_shape=(out, out),
        grid=(b, nb),
        in_specs=[fwd_t(mw), fwd, fwd_t(mw), fwd_t(ng), bwd_t(mw), bwd, bwd_t(mw), bwd_t(ng)],
        out_specs=(fwd, bwd),
        scratch_shapes=[pltpu.VMEM((2 * M_HEADS, CT_ROWS, M_HEAD), F32),
                        pltpu.VMEM((2 * M_HEADS, 8, LANES), F32)],
        compiler_params=_params("parallel", "arbitrary"),
        name="m_scan",
    )(qt, k, vt, gt, qt, k, vt, gt)


def _merge_kernel(x_ref, at_ref, hf_ref, hb_ref, xc_ref, opre_ref, ga_ref, gb_ref,
                  mnw_ref, skip_ref, wa_ref, wb_ref, wo_ref, o_ref):
    hc = hf_ref[...].astype(F32) + hb_ref[...].astype(F32)
    hn = jnp.concatenate(
        [_rms(hc[:, h * M_HEAD:(h + 1) * M_HEAD], mnw_ref[:, h * M_HEAD:(h + 1) * M_HEAD])
         for h in range(M_HEADS)], axis=1)
    bm = jax.nn.sigmoid(opre_ref[...].astype(F32)) * (hn + skip_ref[...] * xc_ref[...].astype(F32))
    ya = lax.dot_general(at_ref[0], wa_ref[...], _TN, preferred_element_type=F32)
    yb = _dot(bm.astype(BF16), wb_ref[...])
    merged = (jax.nn.sigmoid(ga_ref[...].astype(F32)) * ya + jax.nn.sigmoid(gb_ref[...].astype(F32)) * yb)
    o_ref[...] = x_ref[...] + _dot(merged.astype(BF16), wo_ref[...])


def _merge(x, at, hf, hb, xc, opre, ga, gb, mnw, skip, wa, wb, wo, *, b, s, tm=512):
    t, d = x.shape
    mw = hf.shape[1]
    nt = s // tm
    row = lambda w: pl.BlockSpec((tm, w), lambda bi, i: (bi * nt + i, 0))
    return pl.pallas_call(
        _merge_kernel,
        out_shape=jax.ShapeDtypeStruct((t, d), F32),
        grid=(b, nt),
        in_specs=[row(d), pl.BlockSpec((1, at.shape[1], tm), lambda bi, i: (bi, 0, i)),
                  row(mw), row(mw), row(mw), row(mw), row(d), row(d)]
                 + [_resident(a.shape) for a in (mnw, skip, wa, wb, wo)],
        out_specs=row(d),
        compiler_params=_params("parallel", "parallel"),
        name="merge",
    )(x, at, hf, hb, xc, opre, ga, gb, mnw, skip, wa, wb, wo)


def _pad_cols(w, width):
    return jnp.pad(w, ((0, 0), (0, width - w.shape[1])))


def _head_pad(w, head_w):
    r = w.shape[0]
    w = w.reshape(r, MLA_HEADS, head_w)
    return jnp.pad(w, ((0, 0), (0, 0), (0, LANES - head_w))).reshape(r, MLA_HEADS * LANES)


def kernel(x, positions, ffn1_norm_w, ffn1_w_gate, ffn1_w_up, ffn1_w_down, mix_norm_w, w_in, q_a_norm_w, w_uq, kv_a_norm_w, w_uk, w_uv, q_norm_w, k_norm_w, w_branch_a, conv_w, conv_b, w_mq, w_mk, w_mv, b_igate, b_fgate, m_norm_w, m_skip, w_branch_b, w_out, ffn2_norm_w, ffn2_w_gate, ffn2_w_up, ffn2_w_down, final_norm_w):
    b, s, d = x.shape
    depth = w_in.shape[0]
    q_lora, kv_lora = q_a_norm_w.shape[1], kv_a_norm_w.shape[1]
    mw = m_norm_w.shape[1]
    bf = lambda a: a.astype(BF16)
    row = lambda a: a.reshape(1, -1).astype(F32)

    half = QK_ROPE // 2
    inv = (ROPE_THETA ** (-jnp.arange(half, dtype=F32) / half)).reshape(half, 1)
    q_scale = QK_HEAD ** -0.5 * math.log2(math.e)

    xt = x.reshape(b * s, d)
    pos = positions.reshape(1, b * s)
    for l in range(depth):
        xt = _ffn(xt, row(ffn1_norm_w[l]), bf(ffn1_w_gate[l]), bf(ffn1_w_up[l]), bf(ffn1_w_down[l]),
                  row(final_norm_w[l]), final_norm=False)

        w = w_in[l]
        o_lat = q_lora + kv_lora + QK_ROPE
        o_g = o_lat + 2 * mw
        o_ga = o_g + 4 * M_HEADS
        wcat = jnp.concatenate(
            [_pad_cols(w[:, :o_lat], LAT_W), w[:, o_lat:o_lat + mw], w[:, o_lat + mw:o_g],
             w[:, o_ga:o_ga + d], w[:, o_ga + d:]], axis=1)
        gbias = jnp.concatenate([b_igate[l], b_fgate[l]]).reshape(-1, 1).astype(F32)
        lat, m_in, o_pre, g_a, g_b, gt = _proj(xt, row(mix_norm_w[l]), bf(wcat), bf(w[:, o_g:o_ga].T), gbias,
                                               mw=mw)

        qht, kh, vt = _mla_prep(
            lat, pos, row(q_a_norm_w[l]), bf(_head_pad(w_uq[l], QK_HEAD).T), row(kv_a_norm_w[l]),
            bf(_head_pad(w_uk[l], QK_NOPE)), bf(w_uv[l].T), _pad_cols(row(q_norm_w[l]), LANES).T,
            _pad_cols(row(k_norm_w[l]), LANES), inv,
            b=b, s=s, q_lora=q_lora, kv_lora=kv_lora, q_scale=q_scale)
        at = _attn(qht, kh, vt).reshape(b, MLA_HEADS * V_HEAD, s)

        xc, mqt, mk, mvt = _mprep(m_in, conv_w[l], row(conv_b[l]), bf(jnp.swapaxes(w_mq[l], 1, 2)), bf(w_mk[l]),
                                  bf(jnp.swapaxes(w_mv[l], 1, 2)), b=b, s=s)
        hf, hb = _mscan(mqt, mk, mvt, gt, b=b, s=s)

        xt = _merge(xt, at, hf, hb, xc, o_pre, g_a, g_b, row(m_norm_w[l]), row(m_skip[l]),
                    bf(w_branch_a[l]), bf(w_branch_b[l]), bf(w_out[l]), b=b, s=s)
        xt = _ffn(xt, row(ffn2_norm_w[l]), bf(ffn2_w_gate[l]), bf(ffn2_w_up[l]), bf(ffn2_w_down[l]),
                  row(final_norm_w[l]), final_norm=True)
    return xt.reshape(b, s, d)
```

```python
import functools
import math

import jax
import jax.numpy as jnp
from jax import lax
from jax.experimental import pallas as pl
from jax.experimental.pallas import tpu as pltpu

F32 = jnp.float32
BF16 = jnp.bfloat16

EPS = 1e-6
ROPE_THETA = 10000.0
LANES = 128
MLA_HEADS = 8
QK_NOPE = 64
QK_ROPE = 32
QK_HEAD = QK_NOPE + QK_ROPE
V_HEAD = 64
VT_ROWS = 80
M_HEADS = 4
M_HEAD = 128
CONV_K = 5
CHUNK = 256
CT_ROWS = 144
LAT_W = 512
VMEM_LIMIT = 56 * 1024 * 1024

_NT = (((1,), (1,)), ((), ()))
_TN = (((0,), (0,)), ((), ()))


def _rms(x, w):
    ms = jnp.mean(x * x, axis=-1, keepdims=True)
    return x * lax.rsqrt(ms + EPS) * w


def _dot(a, b):
    return jnp.dot(a, b, preferred_element_type=F32)


def _resident(shape):
    nd = len(shape)
    return pl.BlockSpec(shape, lambda *_: (0,) * nd, pipeline_mode=pl.Buffered(1))


def _params(*sem):
    return pltpu.CompilerParams(dimension_semantics=sem, vmem_limit_bytes=VMEM_LIMIT)


def _half_step(x, nw_ref, wg_ref, wu_ref, wd_ref):
    h = _rms(x, nw_ref[...]).astype(BF16)
    g = _dot(h, wg_ref[...])
    u = _dot(h, wu_ref[...])
    a = (g * jax.nn.sigmoid(g) * u).astype(BF16)
    return x + 0.5 * _dot(a, wd_ref[...])


def _ffn_kernel(x_ref, nw_ref, wg_ref, wu_ref, wd_ref, o_ref):
    o_ref[...] = _half_step(x_ref[...], nw_ref, wg_ref, wu_ref, wd_ref)


def _ffn(x, nw, wg, wu, wd, *, tm=512):
    t, d = x.shape
    f = wg.shape[1]
    row = pl.BlockSpec((tm, d), lambda i: (i, 0))
    return pl.pallas_call(
        _ffn_kernel,
        out_shape=jax.ShapeDtypeStruct((t, d), F32),
        grid=(t // tm,),
        in_specs=[row, _resident((1, d)), _resident((d, f)), _resident((d, f)), _resident((f, d))],
        out_specs=row,
        compiler_params=_params("parallel"),
        name="ffn",
    )(x, nw, wg, wu, wd)


def _proj_kernel(x_ref, nw_ref, w_ref, wgt_ref, gbias_ref, lat_ref, min_ref, opre_ref, ga_ref, gb_ref, gt_ref,
                 *, d, mw):
    h = _rms(x_ref[...], nw_ref[...]).astype(BF16)
    o = 0
    for ref, width in ((lat_ref, LAT_W), (min_ref, mw), (opre_ref, mw), (ga_ref, d), (gb_ref, d)):
        ref[...] = _dot(h, w_ref[:, o:o + width]).astype(ref.dtype)
        o += width
    gt_ref[...] = lax.dot_general(wgt_ref[...], h, _NT, preferred_element_type=F32) + gbias_ref[...]


def _proj(x, nw, wcat, wgt, gbias, *, mw, tm=512):
    t, d = x.shape
    n = wcat.shape[1]
    ng = wgt.shape[0]
    row = lambda w: pl.BlockSpec((tm, w), lambda i: (i, 0))
    out_shape = (jax.ShapeDtypeStruct((t, LAT_W), F32), jax.ShapeDtypeStruct((t, mw), F32),
                 jax.ShapeDtypeStruct((t, mw), BF16), jax.ShapeDtypeStruct((t, d), BF16),
                 jax.ShapeDtypeStruct((t, d), BF16), jax.ShapeDtypeStruct((ng, t), F32))
    return pl.pallas_call(
        functools.partial(_proj_kernel, d=d, mw=mw),
        out_shape=out_shape,
        grid=(t // tm,),
        in_specs=[row(d), _resident((1, d)), _resident((d, n)), _resident((ng, d)), _resident((ng, 1))],
        out_specs=(row(LAT_W), row(mw), row(mw), row(d), row(d), pl.BlockSpec((ng, tm), lambda i: (0, i))),
        compiler_params=_params("parallel"),
        name="proj",
    )(x, nw, wcat, wgt, gbias)


def _mla_prep_kernel(lat_ref, pos_ref, qaw_ref, wuqt_ref, kvaw_ref, wuk_ref, wuvt_ref, qnw_ref, knw_ref,
                     inv_ref, qt_ref, k_ref, vt_ref, *, q_lora, kv_lora, q_scale):
    tm = lat_ref.shape[0]
    lat = lat_ref[...]
    cqn = _rms(lat[:, 0:q_lora], qaw_ref[...]).astype(BF16)
    ckvn = _rms(lat[:, q_lora:q_lora + kv_lora], kvaw_ref[...]).astype(BF16)
    kpe = pltpu.roll(lat[:, q_lora + kv_lora:LAT_W], QK_NOPE, axis=1)
    qft = lax.dot_general(wuqt_ref[...], cqn, _NT, preferred_element_type=F32)
    kf = _dot(ckvn, wuk_ref[...])
    vt = lax.dot_general(wuvt_ref[...], ckvn, _NT, preferred_element_type=F32)

    ang = inv_ref[...] * pos_ref[...].astype(F32)
    cos_t, sin_t = jnp.cos(ang), jnp.sin(ang)
    fill = lambda rows, v: jnp.full((rows, tm), v, F32)
    cos = jnp.concatenate([fill(QK_NOPE, 1.0), cos_t, cos_t, fill(LANES - QK_HEAD, 1.0)], axis=0).T
    sin = jnp.concatenate([fill(QK_NOPE, 0.0), -sin_t, sin_t, fill(LANES - QK_HEAD, 0.0)], axis=0).T
    lane = lax.broadcasted_iota(jnp.int32, (tm, LANES), 1)
    first_half = lane < QK_NOPE + QK_ROPE // 2

    def norm_rope(x, w):
        ms = jnp.sum(x * x, axis=-1, keepdims=True) * (1.0 / QK_HEAD)
        y = x * lax.rsqrt(ms + EPS) * w
        partner = jnp.where(first_half, pltpu.roll(y, LANES - QK_ROPE // 2, axis=1),
                            pltpu.roll(y, QK_ROPE // 2, axis=1))
        return y * cos + partner * sin

    def norm_rope_t(xt, w):
        ms = jnp.sum(xt * xt, axis=0, keepdims=True) * (1.0 / QK_HEAD)
        y = xt * lax.rsqrt(ms + EPS) * w
        y1, y2 = y[QK_NOPE:QK_NOPE + QK_ROPE // 2], y[QK_NOPE + QK_ROPE // 2:QK_HEAD]
        return jnp.concatenate([y[0:QK_NOPE], y1 * cos_t - y2 * sin_t, y2 * cos_t + y1 * sin_t, y[QK_HEAD:]], axis=0)

    ones_row = jnp.where(lax.broadcasted_iota(jnp.int32, (VT_ROWS - V_HEAD, tm), 0) == 0, 1.0, 0.0).astype(BF16)
    for h in range(MLA_HEADS):
        sl = slice(h * LANES, (h + 1) * LANES)
        qt_ref[0, h] = (norm_rope_t(qft[sl, :], qnw_ref[...]) * q_scale).astype(BF16)
        k_ref[0, h] = norm_rope(kf[:, sl] + kpe, knw_ref[...]).astype(BF16)
        vt_ref[0, h, 0:V_HEAD, :] = vt[h * V_HEAD:(h + 1) * V_HEAD, :].astype(BF16)
        vt_ref[0, h, V_HEAD:VT_ROWS, :] = ones_row


def _mla_prep(lat, pos, qaw, wuqt, kvaw, wuk, wuvt, qnw, knw, inv, *, b, s, q_lora, kv_lora, q_scale, tm=512):
    nt = s // tm
    hq = MLA_HEADS
    row = lambda w: pl.BlockSpec((tm, w), lambda bi, i: (bi * nt + i, 0))
    qk_spec = pl.BlockSpec((1, hq, tm, LANES), lambda bi, i: (bi, 0, i, 0))
    in_specs = [row(LAT_W), pl.BlockSpec((1, tm), lambda bi, i: (0, bi * nt + i))] + [
        _resident(a.shape) for a in (qaw, wuqt, kvaw, wuk, wuvt, qnw, knw, inv)]
    return pl.pallas_call(
        functools.partial(_mla_prep_kernel, q_lora=q_lora, kv_lora=kv_lora, q_scale=q_scale),
        out_shape=(jax.ShapeDtypeStruct((b, hq, LANES, s), BF16), jax.ShapeDtypeStruct((b, hq, s, LANES), BF16),
                   jax.ShapeDtypeStruct((b, hq, VT_ROWS, s), BF16)),
        grid=(b, nt),
        in_specs=in_specs,
        out_specs=(pl.BlockSpec((1, hq, LANES, tm), lambda bi, i: (bi, 0, 0, i)), qk_spec,
                   pl.BlockSpec((1, hq, VT_ROWS, tm), lambda bi, i: (bi, 0, 0, i))),
        compiler_params=_params("parallel", "parallel"),
        name="mla_prep",
    )(lat, pos, qaw, wuqt, kvaw, wuk, wuvt, qnw, knw, inv)


def _attn_kernel(qt_ref, k_ref, vt_ref, o_ref, s0_scr, s1_scr, *, tk):
    tq = qt_ref.shape[3]
    nk = k_ref.shape[2] // tk
    qt = qt_ref[0, 0]

    def chunk(c):
        return pl.ds(pl.multiple_of(c * tk, tk), tk)

    def scores(c, s_scr):
        st = _dot(k_ref[0, 0, chunk(c), :], qt)
        s_scr[...] = st
        return jnp.max(st, axis=0, keepdims=True)

    def values(c, s_scr, m, cm, acc):
        m_new = jnp.maximum(m, cm)
        p = jnp.exp2(s_scr[...] - m_new).astype(BF16)
        return m_new, acc * jnp.exp2(m - m_new) + _dot(vt_ref[0, 0, :, chunk(c)], p)

    def pair(i, carry):
        m, cm, acc = carry
        cm1 = scores(2 * i + 1, s1_scr)
        m, acc = values(2 * i, s0_scr, m, cm, acc)
        cm0 = scores(2 * i + 2, s0_scr)
        m, acc = values(2 * i + 1, s1_scr, m, cm1, acc)
        return m, cm0, acc

    init = (jnp.full((1, tq), -jnp.inf, F32), scores(0, s0_scr), jnp.zeros((VT_ROWS, tq), F32))
    m, cm, acc = lax.fori_loop(0, nk // 2 - 1, pair, init)
    cm1 = scores(nk - 1, s1_scr)
    m, acc = values(nk - 2, s0_scr, m, cm, acc)
    m, acc = values(nk - 1, s1_scr, m, cm1, acc)
    o_ref[0, 0] = (acc[0:V_HEAD] / acc[V_HEAD:V_HEAD + 1]).astype(o_ref.dtype)


def _attn(qt, k, vt, *, tq=1024, tk=1024):
    b, h, s, _ = k.shape
    assert s % (2 * tk) == 0 and s % tq == 0
    return pl.pallas_call(
        functools.partial(_attn_kernel, tk=tk),
        out_shape=jax.ShapeDtypeStruct((b, h, V_HEAD, s), BF16),
        grid=(b, h, s // tq),
        in_specs=[pl.BlockSpec((1, 1, LANES, tq), lambda bi, hi, i: (bi, hi, 0, i)),
                  pl.BlockSpec((1, 1, s, LANES), lambda bi, hi, i: (bi, hi, 0, 0)),
                  pl.BlockSpec((1, 1, VT_ROWS, s), lambda bi, hi, i: (bi, hi, 0, 0))],
        out_specs=pl.BlockSpec((1, 1, V_HEAD, tq), lambda bi, hi, i: (bi, hi, 0, i)),
        scratch_shapes=[pltpu.VMEM((tk, tq), F32), pltpu.VMEM((tk, tq), F32)],
        compiler_params=_params("parallel", "parallel", "arbitrary"),
        name="attn",
    )(qt, k, vt)


HALO = 8


def _mprep_kernel(cur_ref, prev_ref, next_ref, cw_ref, cb_ref, wqt_ref, wk_ref, wvt_ref,
                  xc_ref, qt_ref, k_ref, vt_ref, xs_ref):
    tm = cur_ref.shape[0]
    i = pl.program_id(1)
    cur = cur_ref[...]
    xs_ref[0:HALO, :] = jnp.where(i > 0, prev_ref[...], 0.0)
    xs_ref[HALO:HALO + tm, :] = cur
    xs_ref[HALO + tm:2 * HALO + tm, :] = jnp.where(i < pl.num_programs(1) - 1, next_ref[...], 0.0)
    y = cb_ref[...]
    for kk in range(CONV_K):
        y = y + cw_ref[kk:kk + 1, :] * xs_ref[pl.ds(HALO - CONV_K // 2 + kk, tm), :]
    xc = (y * jax.nn.sigmoid(y)).astype(BF16)
    xc_ref[...] = xc
    xm = cur.astype(BF16)
    for h in range(M_HEADS):
        sl = slice(h * M_HEAD, (h + 1) * M_HEAD)
        qt_ref[sl, :] = lax.dot_general(wqt_ref[h], xc[:, sl], _NT, preferred_element_type=F32).astype(BF16)
        k_ref[:, sl] = (_dot(xc[:, sl], wk_ref[h]) * (M_HEAD ** -0.5)).astype(BF16)
        vt_ref[sl, :] = lax.dot_general(wvt_ref[h], xm[:, sl], _NT, preferred_element_type=F32).astype(BF16)


def _mprep(m_in, cw, cb, wqt, wk, wvt, *, b, s, tm=512):
    t, mw = m_in.shape
    nt = s // tm
    hb = tm // HALO
    row = pl.BlockSpec((tm, mw), lambda bi, i: (bi * nt + i, 0))
    prev = pl.BlockSpec((HALO, mw), lambda bi, i: (jnp.maximum((bi * nt + i) * hb - 1, 0), 0))
    nxt = pl.BlockSpec((HALO, mw), lambda bi, i: (jnp.minimum((bi * nt + i + 1) * hb, t // HALO - 1), 0))
    col = pl.BlockSpec((mw, tm), lambda bi, i: (0, bi * nt + i))
    out = jax.ShapeDtypeStruct((t, mw), BF16)
    out_t = jax.ShapeDtypeStruct((mw, t), BF16)
    return pl.pallas_call(
        _mprep_kernel,
        out_shape=(out, out_t, out, out_t),
        grid=(b, nt),
        in_specs=[row, prev, nxt] + [_resident(a.shape) for a in (cw, cb, wqt, wk, wvt)],
        out_specs=(row, col, row, col),
        scratch_shapes=[pltpu.VMEM((tm + 2 * HALO, mw), F32)],
        compiler_params=_params("parallel", "parallel"),
        name="m_prep",
    )(m_in, m_in, m_in, cw, cb, wqt, wk, wvt)


def _log_sigmoid(x):
    return jnp.minimum(x, 0.0) - jnp.log1p(jnp.exp(-jnp.abs(x)))


def _cumsum_lanes(x, tri):
    n = x.shape[0]
    hi = x.astype(BF16)
    r1 = x - hi.astype(F32)
    mid = r1.astype(BF16)
    lo = (r1 - mid.astype(F32)).astype(BF16)
    y = _dot(jnp.concatenate([hi, mid, lo], axis=0), tri)
    return y[0:n] + y[n:2 * n] + y[2 * n:3 * n]


def _mscan_kernel(qf_ref, kf_ref, vf_ref, gf_ref, qb_ref, kb_ref, vb_ref, gb_ref,
                  hf_ref, hb_ref, c_scr, m_scr, *, sub):
    L = CHUNK
    ng = 2 * M_HEADS

    @pl.when(pl.program_id(1) == 0)
    def _():
        c_scr[...] = jnp.zeros_like(c_scr)
        m_scr[...] = jnp.zeros_like(m_scr)

    row = lax.broadcasted_iota(jnp.int32, (L, L), 0)
    col = lax.broadcasted_iota(jnp.int32, (L, L), 1)
    visible = (row <= col, row >= col)
    tri = tuple(jnp.where(v, 1.0, 0.0).astype(BF16) for v in visible)
    ones_rows = jnp.where(lax.broadcasted_iota(jnp.int32, (CT_ROWS - M_HEAD, L), 0) == 0, 1.0, 0.0).astype(BF16)
    total = (L - 1, 0)
    dirs = ((qf_ref, kf_ref, vf_ref, gf_ref, hf_ref), (qb_ref, kb_ref, vb_ref, gb_ref, hb_ref))

    for c in range(sub):
        for d, (q_ref, k_ref, v_ref, g_ref, h_ref) in enumerate(dirs):
            lo = (c if d == 0 else sub - 1 - c) * L
            tok = slice(lo, lo + L)
            gt = g_ref[:, tok]
            bt = _cumsum_lanes(_log_sigmoid(gt), tri[d])
            r8 = gt[0:ng] - bt[ng:2 * ng]
            rc = jnp.concatenate([r8, jnp.zeros((LANES - ng, L), F32)], axis=0).T
            for h in range(M_HEADS):
                ch = d * M_HEADS + h
                hs = slice(h * M_HEAD, (h + 1) * M_HEAD)
                qt, kc = q_ref[hs, tok], k_ref[tok, hs]
                v_ext = jnp.concatenate([v_ref[hs, tok], ones_rows], axis=0)
                b_row = bt[ng + ch:ng + ch + 1, :]
                r_row = r8[ch:ch + 1, :]
                b_tot = b_row[:, total[d]:total[d] + 1]
                m = m_scr[ch, 0:1, 0:1]
                cst = c_scr[ch]

                logw = jnp.where(visible[d], b_row + rc[:, ch:ch + 1], -jnp.inf)
                mx = jnp.max(logw, axis=0, keepdims=True)
                kq = _dot(jnp.concatenate([kc, cst.astype(BF16)], axis=0), qt)
                intra = _dot(v_ext, (kq[0:L] * jnp.exp(logw - mx)).astype(BF16))
                g_row = b_tot + r_row
                g_max = jnp.max(g_row, axis=1, keepdims=True)
                incr = _dot((v_ext.astype(F32) * jnp.exp(g_row - g_max)).astype(BF16), kc)

                inter = b_row + m
                m_t = jnp.maximum(mx, inter)
                tot = jnp.exp(mx - m_t) * intra + jnp.exp(inter - m_t) * kq[L:L + CT_ROWS]
                den = jnp.maximum(jnp.abs(tot[M_HEAD:M_HEAD + 1]), jnp.exp(-m_t))
                h_ref[tok, hs] = (tot[0:M_HEAD] * (1.0 / den)).T.astype(h_ref.dtype)
                m_new = jnp.maximum(b_tot + m, g_max)
                c_scr[ch] = jnp.exp(b_tot + m - m_new) * cst + jnp.exp(g_max - m_new) * incr
                m_scr[ch] = jnp.broadcast_to(m_new, m_scr.shape[1:])


def _mscan(qt, k, vt, gt, *, b, s, sub=2):
    t, mw = k.shape
    ng = gt.shape[0]
    blk = sub * CHUNK
    nb = s // blk
    fwd = pl.BlockSpec((blk, mw), lambda bi, j: (bi * nb + j, 0))
    bwd = pl.BlockSpec((blk, mw), lambda bi, j: (bi * nb + nb - 1 - j, 0))
    fwd_t = lambda r: pl.BlockSpec((r, blk), lambda bi, j: (0, bi * nb + j))
    bwd_t = lambda r: pl.BlockSpec((r, blk), lambda bi, j: (0, bi * nb + nb - 1 - j))
    out = jax.ShapeDtypeStruct((t, mw), BF16)
    return pl.pallas_call(
        functools.partial(_mscan_kernel, sub=sub),
        out_shape=(out, out),
        grid=(b, nb),
        in_specs=[fwd_t(mw), fwd, fwd_t(mw), fwd_t(ng), bwd_t(mw), bwd, bwd_t(mw), bwd_t(ng)],
        out_specs=(fwd, bwd),
        scratch_shapes=[pltpu.VMEM((2 * M_HEADS, CT_ROWS, M_HEAD), F32),
                        pltpu.VMEM((2 * M_HEADS, 8, LANES), F32)],
        compiler_params=_params("parallel", "arbitrary"),
        name="m_scan",
    )(qt, k, vt, gt, qt, k, vt, gt)


def _merge_kernel(x_ref, at_ref, hf_ref, hb_ref, xc_ref, opre_ref, ga_ref, gb_ref,
                  mnw_ref, skip_ref, wa_ref, wb_ref, wo_ref, nw_ref, wg_ref, wu_ref, wd_ref, fw_ref, o_ref):
    hc = hf_ref[...].astype(F32) + hb_ref[...].astype(F32)
    hn = jnp.concatenate(
        [_rms(hc[:, h * M_HEAD:(h + 1) * M_HEAD], mnw_ref[:, h * M_HEAD:(h + 1) * M_HEAD])
         for h in range(M_HEADS)], axis=1)
    bm = jax.nn.sigmoid(opre_ref[...].astype(F32)) * (hn + skip_ref[...] * xc_ref[...].astype(F32))
    ya = lax.dot_general(at_ref[0], wa_ref[...], _TN, preferred_element_type=F32)
    yb = _dot(bm.astype(BF16), wb_ref[...])
    merged = (jax.nn.sigmoid(ga_ref[...].astype(F32)) * ya + jax.nn.sigmoid(gb_ref[...].astype(F32)) * yb)
    x = x_ref[...] + _dot(merged.astype(BF16), wo_ref[...])
    o_ref[...] = _rms(_half_step(x, nw_ref, wg_ref, wu_ref, wd_ref), fw_ref[...])


def _merge(x, at, hf, hb, xc, opre, ga, gb, mnw, skip, wa, wb, wo, nw, wg, wu, wd, fw, *, b, s, tm=512):
    t, d = x.shape
    mw = hf.shape[1]
    nt = s // tm
    row = lambda w: pl.BlockSpec((tm, w), lambda bi, i: (bi * nt + i, 0))
    return pl.pallas_call(
        _merge_kernel,
        out_shape=jax.ShapeDtypeStruct((t, d), F32),
        grid=(b, nt),
        in_specs=[row(d), pl.BlockSpec((1, at.shape[1], tm), lambda bi, i: (bi, 0, i)),
                  row(mw), row(mw), row(mw), row(mw), row(d), row(d)]
                 + [_resident(a.shape) for a in (mnw, skip, wa, wb, wo, nw, wg, wu, wd, fw)],
        out_specs=row(d),
        compiler_params=_params("parallel", "parallel"),
        name="merge",
    )(x, at, hf, hb, xc, opre, ga, gb, mnw, skip, wa, wb, wo, nw, wg, wu, wd, fw)


def _pad_cols(w, width):
    return jnp.pad(w, ((0, 0), (0, width - w.shape[1])))


def _head_pad(w, head_w):
    r = w.shape[0]
    w = w.reshape(r, MLA_HEADS, head_w)
    return jnp.pad(w, ((0, 0), (0, 0), (0, LANES - head_w))).reshape(r, MLA_HEADS * LANES)


def kernel(x, positions, ffn1_norm_w, ffn1_w_gate, ffn1_w_up, ffn1_w_down, mix_norm_w, w_in, q_a_norm_w, w_uq, kv_a_norm_w, w_uk, w_uv, q_norm_w, k_norm_w, w_branch_a, conv_w, conv_b, w_mq, w_mk, w_mv, b_igate, b_fgate, m_norm_w, m_skip, w_branch_b, w_out, ffn2_norm_w, ffn2_w_gate, ffn2_w_up, ffn2_w_down, final_norm_w):
    b, s, d = x.shape
    depth = w_in.shape[0]
    q_lora, kv_lora = q_a_norm_w.shape[1], kv_a_norm_w.shape[1]
    mw = m_norm_w.shape[1]
    bf = lambda a: a.astype(BF16)
    row = lambda a: a.reshape(1, -1).astype(F32)

    half = QK_ROPE // 2
    inv = (ROPE_THETA ** (-jnp.arange(half, dtype=F32) / half)).reshape(half, 1)
    q_scale = QK_HEAD ** -0.5 * math.log2(math.e)

    xt = x.reshape(b * s, d)
    pos = positions.reshape(1, b * s)
    for l in range(depth):
        xt = _ffn(xt, row(ffn1_norm_w[l]), bf(ffn1_w_gate[l]), bf(ffn1_w_up[l]), bf(ffn1_w_down[l]))

        w = w_in[l]
        o_lat = q_lora + kv_lora + QK_ROPE
        o_g = o_lat + 2 * mw
        o_ga = o_g + 4 * M_HEADS
        wcat = jnp.concatenate(
            [_pad_cols(w[:, :o_lat], LAT_W), w[:, o_lat:o_lat + mw], w[:, o_lat + mw:o_g],
             w[:, o_ga:o_ga + d], w[:, o_ga + d:]], axis=1)
        gbias = jnp.concatenate([b_igate[l], b_fgate[l]]).reshape(-1, 1).astype(F32)
        lat, m_in, o_pre, g_a, g_b, gt = _proj(xt, row(mix_norm_w[l]), bf(wcat), bf(w[:, o_g:o_ga].T), gbias,
                                               mw=mw)

        qht, kh, vt = _mla_prep(
            lat, pos, row(q_a_norm_w[l]), bf(_head_pad(w_uq[l], QK_HEAD).T), row(kv_a_norm_w[l]),
            bf(_head_pad(w_uk[l], QK_NOPE)), bf(w_uv[l].T), _pad_cols(row(q_norm_w[l]), LANES).T,
            _pad_cols(row(k_norm_w[l]), LANES), inv,
            b=b, s=s, q_lora=q_lora, kv_lora=kv_lora, q_scale=q_scale)
        at = _attn(qht, kh, vt).reshape(b, MLA_HEADS * V_HEAD, s)

        xc, mqt, mk, mvt = _mprep(m_in, conv_w[l], row(conv_b[l]), bf(jnp.swapaxes(w_mq[l], 1, 2)), bf(w_mk[l]),
                                  bf(jnp.swapaxes(w_mv[l], 1, 2)), b=b, s=s)
        hf, hb = _mscan(mqt, mk, mvt, gt, b=b, s=s)

        xt = _merge(xt, at, hf, hb, xc, o_pre, g_a, g_b, row(m_norm_w[l]), row(m_skip[l]),
                    bf(w_branch_a[l]), bf(w_branch_b[l]), bf(w_out[l]), row(ffn2_norm_w[l]), bf(ffn2_w_gate[l]),
                    bf(ffn2_w_up[l]), bf(ffn2_w_down[l]), row(final_norm_w[l]), b=b, s=s)
    return xt.reshape(b, s, d)
```

```python
import functools
import math

import jax
import jax.numpy as jnp
from jax import lax
from jax.experimental import pallas as pl
from jax.experimental.pallas import tpu as pltpu

F32 = jnp.float32
BF16 = jnp.bfloat16

EPS = 1e-6
ROPE_THETA = 10000.0
LANES = 128
MLA_HEADS = 8
QK_NOPE = 64
QK_ROPE = 32
QK_HEAD = QK_NOPE + QK_ROPE
V_HEAD = 64
VT_ROWS = 80
M_HEADS = 4
M_HEAD = 128
CONV_K = 5
CHUNK = 256
CT_ROWS = 144
LAT_W = 512
VMEM_LIMIT = 56 * 1024 * 1024

_NT = (((1,), (1,)), ((), ()))
_TN = (((0,), (0,)), ((), ()))


def _rms(x, w):
    ms = jnp.mean(x * x, axis=-1, keepdims=True)
    return x * lax.rsqrt(ms + EPS) * w


def _dot(a, b):
    return jnp.dot(a, b, preferred_element_type=F32)


def _resident(shape):
    nd = len(shape)
    return pl.BlockSpec(shape, lambda *_: (0,) * nd, pipeline_mode=pl.Buffered(1))


def _params(*sem):
    return pltpu.CompilerParams(dimension_semantics=sem, vmem_limit_bytes=VMEM_LIMIT)


def _half_step(x, nw_ref, wg_ref, wu_ref, wd_ref):
    h = _rms(x, nw_ref[...]).astype(BF16)
    g = _dot(h, wg_ref[...])
    u = _dot(h, wu_ref[...])
    a = (g * jax.nn.sigmoid(g) * u).astype(BF16)
    return x + 0.5 * _dot(a, wd_ref[...])


def _ffn_kernel(x_ref, nw_ref, wg_ref, wu_ref, wd_ref, o_ref):
    o_ref[...] = _half_step(x_ref[...], nw_ref, wg_ref, wu_ref, wd_ref)


def _ffn(x, nw, wg, wu, wd, *, tm=512):
    t, d = x.shape
    f = wg.shape[1]
    row = pl.BlockSpec((tm, d), lambda i: (i, 0))
    return pl.pallas_call(
        _ffn_kernel,
        out_shape=jax.ShapeDtypeStruct((t, d), F32),
        grid=(t // tm,),
        in_specs=[row, _resident((1, d)), _resident((d, f)), _resident((d, f)), _resident((f, d))],
        out_specs=row,
        compiler_params=_params("parallel"),
        name="ffn",
    )(x, nw, wg, wu, wd)


def _proj_kernel(x_ref, nw_ref, w_ref, wgt_ref, gbias_ref, lat_ref, min_ref, opre_ref, ga_ref, gb_ref, gt_ref,
                 *, d, mw):
    h = _rms(x_ref[...], nw_ref[...]).astype(BF16)
    o = 0
    for ref, width in ((lat_ref, LAT_W), (min_ref, mw), (opre_ref, mw), (ga_ref, d), (gb_ref, d)):
        ref[...] = _dot(h, w_ref[:, o:o + width]).astype(ref.dtype)
        o += width
    gt_ref[...] = lax.dot_general(wgt_ref[...], h, _NT, preferred_element_type=F32) + gbias_ref[...]


def _proj(x, nw, wcat, wgt, gbias, *, mw, tm=512):
    t, d = x.shape
    n = wcat.shape[1]
    ng = wgt.shape[0]
    row = lambda w: pl.BlockSpec((tm, w), lambda i: (i, 0))
    out_shape = (jax.ShapeDtypeStruct((t, LAT_W), F32), jax.ShapeDtypeStruct((t, mw), F32),
                 jax.ShapeDtypeStruct((t, mw), BF16), jax.ShapeDtypeStruct((t, d), BF16),
                 jax.ShapeDtypeStruct((t, d), BF16), jax.ShapeDtypeStruct((ng, t), F32))
    return pl.pallas_call(
        functools.partial(_proj_kernel, d=d, mw=mw),
        out_shape=out_shape,
        grid=(t // tm,),
        in_specs=[row(d), _resident((1, d)), _resident((d, n)), _resident((ng, d)), _resident((ng, 1))],
        out_specs=(row(LAT_W), row(mw), row(mw), row(d), row(d), pl.BlockSpec((ng, tm), lambda i: (0, i))),
        compiler_params=_params("parallel"),
        name="proj",
    )(x, nw, wcat, wgt, gbias)


def _mla_prep_kernel(lat_ref, pos_ref, qaw_ref, wuqt_ref, kvaw_ref, wuk_ref, wuvt_ref, qnw_ref, knw_ref,
                     inv_ref, qt_ref, k_ref, vt_ref, *, q_lora, kv_lora, q_scale):
    tm = lat_ref.shape[0]
    lat = lat_ref[...]
    cqn = _rms(lat[:, 0:q_lora], qaw_ref[...]).astype(BF16)
    ckvn = _rms(lat[:, q_lora:q_lora + kv_lora], kvaw_ref[...]).astype(BF16)
    kpe = pltpu.roll(lat[:, q_lora + kv_lora:LAT_W], QK_NOPE, axis=1)
    qft = lax.dot_general(wuqt_ref[...], cqn, _NT, preferred_element_type=F32)
    kf = _dot(ckvn, wuk_ref[...])
    vt = lax.dot_general(wuvt_ref[...], ckvn, _NT, preferred_element_type=F32)

    ang = inv_ref[...] * pos_ref[...].astype(F32)
    cos_t, sin_t = jnp.cos(ang), jnp.sin(ang)
    fill = lambda rows, v: jnp.full((rows, tm), v, F32)
    cos = jnp.concatenate([fill(QK_NOPE, 1.0), cos_t, cos_t, fill(LANES - QK_HEAD, 1.0)], axis=0).T
    sin = jnp.concatenate([fill(QK_NOPE, 0.0), -sin_t, sin_t, fill(LANES - QK_HEAD, 0.0)], axis=0).T
    lane = lax.broadcasted_iota(jnp.int32, (tm, LANES), 1)
    first_half = lane < QK_NOPE + QK_ROPE // 2

    def norm_rope(x, w):
        ms = jnp.sum(x * x, axis=-1, keepdims=True) * (1.0 / QK_HEAD)
        y = x * lax.rsqrt(ms + EPS) * w
        partner = jnp.where(first_half, pltpu.roll(y, LANES - QK_ROPE // 2, axis=1),
                            pltpu.roll(y, QK_ROPE // 2, axis=1))
        return y * cos + partner * sin

    def norm_rope_t(xt, w):
        ms = jnp.sum(xt * xt, axis=0, keepdims=True) * (1.0 / QK_HEAD)
        y = xt * lax.rsqrt(ms + EPS) * w
        y1, y2 = y[QK_NOPE:QK_NOPE + QK_ROPE // 2], y[QK_NOPE + QK_ROPE // 2:QK_HEAD]
        return jnp.concatenate([y[0:QK_NOPE], y1 * cos_t - y2 * sin_t, y2 * cos_t + y1 * sin_t, y[QK_HEAD:]], axis=0)

    ones_row = jnp.where(lax.broadcasted_iota(jnp.int32, (VT_ROWS - V_HEAD, tm), 0) == 0, 1.0, 0.0).astype(BF16)
    for h in range(MLA_HEADS):
        sl = slice(h * LANES, (h + 1) * LANES)
        qt_ref[0, h] = (norm_rope_t(qft[sl, :], qnw_ref[...]) * q_scale).astype(BF16)
        k_ref[0, h] = norm_rope(kf[:, sl] + kpe, knw_ref[...]).astype(BF16)
        vt_ref[0, h, 0:V_HEAD, :] = vt[h * V_HEAD:(h + 1) * V_HEAD, :].astype(BF16)
        vt_ref[0, h, V_HEAD:VT_ROWS, :] = ones_row


def _mla_prep(lat, pos, qaw, wuqt, kvaw, wuk, wuvt, qnw, knw, inv, *, b, s, q_lora, kv_lora, q_scale, tm=512):
    nt = s // tm
    hq = MLA_HEADS
    row = lambda w: pl.BlockSpec((tm, w), lambda bi, i: (bi * nt + i, 0))
    qk_spec = pl.BlockSpec((1, hq, tm, LANES), lambda bi, i: (bi, 0, i, 0))
    in_specs = [row(LAT_W), pl.BlockSpec((1, tm), lambda bi, i: (0, bi * nt + i))] + [
        _resident(a.shape) for a in (qaw, wuqt, kvaw, wuk, wuvt, qnw, knw, inv)]
    return pl.pallas_call(
        functools.partial(_mla_prep_kernel, q_lora=q_lora, kv_lora=kv_lora, q_scale=q_scale),
        out_shape=(jax.ShapeDtypeStruct((b, hq, LANES, s), BF16), jax.ShapeDtypeStruct((b, hq, s, LANES), BF16),
                   jax.ShapeDtypeStruct((b, hq, VT_ROWS, s), BF16)),
        grid=(b, nt),
        in_specs=in_specs,
        out_specs=(pl.BlockSpec((1, hq, LANES, tm), lambda bi, i: (bi, 0, 0, i)), qk_spec,
                   pl.BlockSpec((1, hq, VT_ROWS, tm), lambda bi, i: (bi, 0, 0, i))),
        compiler_params=_params("parallel", "parallel"),
        name="mla_prep",
    )(lat, pos, qaw, wuqt, kvaw, wuk, wuvt, qnw, knw, inv)


def _attn_kernel(qt_ref, k_ref, vt_ref, o_ref, s0_scr, s1_scr, *, tk):
    tq = qt_ref.shape[3]
    nk = k_ref.shape[2] // tk
    qt = qt_ref[0, 0]

    def chunk(c):
        return pl.ds(pl.multiple_of(c * tk, tk), tk)

    def scores(c, s_scr):
        st = _dot(k_ref[0, 0, chunk(c), :], qt)
        s_scr[...] = st
        return jnp.max(st, axis=0, keepdims=True)

    def values(c, s_scr, m, cm, acc):
        m_new = jnp.maximum(m, cm)
        p = jnp.exp2(s_scr[...] - m_new).astype(BF16)
        return m_new, acc * jnp.exp2(m - m_new) + _dot(vt_ref[0, 0, :, chunk(c)], p)

    def pair(i, carry):
        m, cm, acc = carry
        cm1 = scores(2 * i + 1, s1_scr)
        m, acc = values(2 * i, s0_scr, m, cm, acc)
        cm0 = scores(2 * i + 2, s0_scr)
        m, acc = values(2 * i + 1, s1_scr, m, cm1, acc)
        return m, cm0, acc

    init = (jnp.full((1, tq), -jnp.inf, F32), scores(0, s0_scr), jnp.zeros((VT_ROWS, tq), F32))
    m, cm, acc = lax.fori_loop(0, nk // 2 - 1, pair, init)
    cm1 = scores(nk - 1, s1_scr)
    m, acc = values(nk - 2, s0_scr, m, cm, acc)
    m, acc = values(nk - 1, s1_scr, m, cm1, acc)
    o_ref[0, 0] = (acc[0:V_HEAD] / acc[V_HEAD:V_HEAD + 1]).astype(o_ref.dtype)


def _attn(qt, k, vt, *, tq=2048, tk=1024):
    b, h, s, _ = k.shape
    assert s % (2 * tk) == 0 and s % tq == 0
    return pl.pallas_call(
        functools.partial(_attn_kernel, tk=tk),
        out_shape=jax.ShapeDtypeStruct((b, h, V_HEAD, s), BF16),
        grid=(b, h, s // tq),
        in_specs=[pl.BlockSpec((1, 1, LANES, tq), lambda bi, hi, i: (bi, hi, 0, i)),
                  pl.BlockSpec((1, 1, s, LANES), lambda bi, hi, i: (bi, hi, 0, 0)),
                  pl.BlockSpec((1, 1, VT_ROWS, s), lambda bi, hi, i: (bi, hi, 0, 0))],
        out_specs=pl.BlockSpec((1, 1, V_HEAD, tq), lambda bi, hi, i: (bi, hi, 0, i)),
        scratch_shapes=[pltpu.VMEM((tk, tq), F32), pltpu.VMEM((tk, tq), F32)],
        compiler_params=_params("parallel", "parallel", "arbitrary"),
        name="attn",
    )(qt, k, vt)


HALO = 8


def _mprep_kernel(cur_ref, prev_ref, next_ref, cw_ref, cb_ref, wqt_ref, wk_ref, wvt_ref,
                  xc_ref, qt_ref, k_ref, vt_ref, xs_ref):
    tm = cur_ref.shape[0]
    i = pl.program_id(1)
    cur = cur_ref[...]
    xs_ref[0:HALO, :] = jnp.where(i > 0, prev_ref[...], 0.0)
    xs_ref[HALO:HALO + tm, :] = cur
    xs_ref[HALO + tm:2 * HALO + tm, :] = jnp.where(i < pl.num_programs(1) - 1, next_ref[...], 0.0)
    y = cb_ref[...]
    for kk in range(CONV_K):
        y = y + cw_ref[kk:kk + 1, :] * xs_ref[pl.ds(HALO - CONV_K // 2 + kk, tm), :]
    xc = (y * jax.nn.sigmoid(y)).astype(BF16)
    xc_ref[...] = xc
    xm = cur.astype(BF16)
    for h in range(M_HEADS):
        sl = slice(h * M_HEAD, (h + 1) * M_HEAD)
        qt_ref[sl, :] = lax.dot_general(wqt_ref[h], xc[:, sl], _NT, preferred_element_type=F32).astype(BF16)
        k_ref[:, sl] = (_dot(xc[:, sl], wk_ref[h]) * (M_HEAD ** -0.5)).astype(BF16)
        vt_ref[sl, :] = lax.dot_general(wvt_ref[h], xm[:, sl], _NT, preferred_element_type=F32).astype(BF16)


def _mprep(m_in, cw, cb, wqt, wk, wvt, *, b, s, tm=512):
    t, mw = m_in.shape
    nt = s // tm
    hb = tm // HALO
    row = pl.BlockSpec((tm, mw), lambda bi, i: (bi * nt + i, 0))
    prev = pl.BlockSpec((HALO, mw), lambda bi, i: (jnp.maximum((bi * nt + i) * hb - 1, 0), 0))
    nxt = pl.BlockSpec((HALO, mw), lambda bi, i: (jnp.minimum((bi * nt + i + 1) * hb, t // HALO - 1), 0))
    col = pl.BlockSpec((mw, tm), lambda bi, i: (0, bi * nt + i))
    out = jax.ShapeDtypeStruct((t, mw), BF16)
    out_t = jax.ShapeDtypeStruct((mw, t), BF16)
    return pl.pallas_call(
        _mprep_kernel,
        out_shape=(out, out_t, out, out_t),
        grid=(b, nt),
        in_specs=[row, prev, nxt] + [_resident(a.shape) for a in (cw, cb, wqt, wk, wvt)],
        out_specs=(row, col, row, col),
        scratch_shapes=[pltpu.VMEM((tm + 2 * HALO, mw), F32)],
        compiler_params=_params("parallel", "parallel"),
        name="m_prep",
    )(m_in, m_in, m_in, cw, cb, wqt, wk, wvt)


def _log_sigmoid(x):
    return jnp.minimum(x, 0.0) - jnp.log1p(jnp.exp(-jnp.abs(x)))


def _cumsum_lanes(x, tri):
    n = x.shape[0]
    hi = x.astype(BF16)
    r1 = x - hi.astype(F32)
    mid = r1.astype(BF16)
    lo = (r1 - mid.astype(F32)).astype(BF16)
    y = _dot(jnp.concatenate([hi, mid, lo], axis=0), tri)
    return y[0:n] + y[n:2 * n] + y[2 * n:3 * n]


def _mscan_kernel(qf_ref, kf_ref, vf_ref, gf_ref, qb_ref, kb_ref, vb_ref, gb_ref,
                  hf_ref, hb_ref, c_scr, m_scr, *, sub):
    L = CHUNK
    ng = 2 * M_HEADS

    @pl.when(pl.program_id(1) == 0)
    def _():
        c_scr[...] = jnp.zeros_like(c_scr)
        m_scr[...] = jnp.zeros_like(m_scr)

    row = lax.broadcasted_iota(jnp.int32, (L, L), 0)
    col = lax.broadcasted_iota(jnp.int32, (L, L), 1)
    visible = (row <= col, row >= col)
    tri = tuple(jnp.where(v, 1.0, 0.0).astype(BF16) for v in visible)
    ones_rows = jnp.where(lax.broadcasted_iota(jnp.int32, (CT_ROWS - M_HEAD, L), 0) == 0, 1.0, 0.0).astype(BF16)
    total = (L - 1, 0)
    dirs = ((qf_ref, kf_ref, vf_ref, gf_ref, hf_ref), (qb_ref, kb_ref, vb_ref, gb_ref, hb_ref))

    for c in range(sub):
        for d, (q_ref, k_ref, v_ref, g_ref, h_ref) in enumerate(dirs):
            lo = (c if d == 0 else sub - 1 - c) * L
            tok = slice(lo, lo + L)
            gt = g_ref[:, tok]
            bt = _cumsum_lanes(_log_sigmoid(gt), tri[d])
            r8 = gt[0:ng] - bt[ng:2 * ng]
            rc = jnp.concatenate([r8, jnp.zeros((LANES - ng, L), F32)], axis=0).T
            for h in range(M_HEADS):
                ch = d * M_HEADS + h
                hs = slice(h * M_HEAD, (h + 1) * M_HEAD)
                qt, kc = q_ref[hs, tok], k_ref[tok, hs]
                v_ext = jnp.concatenate([v_ref[hs, tok], ones_rows], axis=0)
                b_row = bt[ng + ch:ng + ch + 1, :]
                r_row = r8[ch:ch + 1, :]
                b_tot = b_row[:, total[d]:total[d] + 1]
                m = m_scr[ch, 0:1, 0:1]
                cst = c_scr[ch]

                logw = jnp.where(visible[d], b_row + rc[:, ch:ch + 1], -jnp.inf)
                mx = jnp.max(logw, axis=0, keepdims=True)
                kq = _dot(jnp.concatenate([kc, cst.astype(BF16)], axis=0), qt)
                intra = _dot(v_ext, (kq[0:L] * jnp.exp(logw - mx)).astype(BF16))
                g_row = b_tot + r_row
                g_max = jnp.max(g_row, axis=1, keepdims=True)
                incr = _dot((v_ext.astype(F32) * jnp.exp(g_row - g_max)).astype(BF16), kc)

                inter = b_row + m
                m_t = jnp.maximum(mx, inter)
                tot = jnp.exp(mx - m_t) * intra + jnp.exp(inter - m_t) * kq[L:L + CT_ROWS]
                den = jnp.maximum(jnp.abs(tot[M_HEAD:M_HEAD + 1]), jnp.exp(-m_t))
                h_ref[tok, hs] = (tot[0:M_HEAD] * (1.0 / den)).T.astype(h_ref.dtype)
                m_new = jnp.maximum(b_tot + m, g_max)
                c_scr[ch] = jnp.exp(b_tot + m - m_new) * cst + jnp.exp(g_max - m_new) * incr
                m_scr[ch] = jnp.broadcast_to(m_new, m_scr.shape[1:])


def _mscan(qt, k, vt, gt, *, b, s, sub=4):
    t, mw = k.shape
    ng = gt.shape[0]
    blk = sub * CHUNK
    nb = s // blk
    fwd = pl.BlockSpec((blk, mw), lambda bi, j: (bi * nb + j, 0))
    bwd = pl.BlockSpec((blk, mw), lambda bi, j: (bi * nb + nb - 1 - j, 0))
    fwd_t = lambda r: pl.BlockSpec((r, blk), lambda bi, j: (0, bi * nb + j))
    bwd_t = lambda r: pl.BlockSpec((r, blk), lambda bi, j: (0, bi * nb + nb - 1 - j))
    out = jax.ShapeDtypeStruct((t, mw), BF16)
    return pl.pallas_call(
        functools.partial(_mscan_kernel, sub=sub),
        out_shape=(out, out),
        grid=(b, nb),
        in_specs=[fwd_t(mw), fwd, fwd_t(mw), fwd_t(ng), bwd_t(mw), bwd, bwd_t(mw), bwd_t(ng)],
        out_specs=(fwd, bwd),
        scratch_shapes=[pltpu.VMEM((2 * M_HEADS, CT_ROWS, M_HEAD), F32),
                        pltpu.VMEM((2 * M_HEADS, 8, LANES), F32)],
        compiler_params=_params("parallel", "arbitrary"),
        name="m_scan",
    )(qt, k, vt, gt, qt, k, vt, gt)


def _merge_kernel(x_ref, at_ref, hf_ref, hb_ref, xc_ref, opre_ref, ga_ref, gb_ref,
                  mnw_ref, skip_ref, wa_ref, wb_ref, wo_ref, nw_ref, wg_ref, wu_ref, wd_ref, fw_ref, o_ref):
    hc = hf_ref[...].astype(F32) + hb_ref[...].astype(F32)
    hn = jnp.concatenate(
        [_rms(hc[:, h * M_HEAD:(h + 1) * M_HEAD], mnw_ref[:, h * M_HEAD:(h + 1) * M_HEAD])
         for h in range(M_HEADS)], axis=1)
    bm = jax.nn.sigmoid(opre_ref[...].astype(F32)) * (hn + skip_ref[...] * xc_ref[...].astype(F32))
    ya = lax.dot_general(at_ref[0], wa_ref[...], _TN, preferred_element_type=F32)
    yb = _dot(bm.astype(BF16), wb_ref[...])
    merged = (jax.nn.sigmoid(ga_ref[...].astype(F32)) * ya + jax.nn.sigmoid(gb_ref[...].astype(F32)) * yb)
    x = x_ref[...] + _dot(merged.astype(BF16), wo_ref[...])
    o_ref[...] = _rms(_half_step(x, nw_ref, wg_ref, wu_ref, wd_ref), fw_ref[...])


def _merge(x, at, hf, hb, xc, opre, ga, gb, mnw, skip, wa, wb, wo, nw, wg, wu, wd, fw, *, b, s, tm=512):
    t, d = x.shape
    mw = hf.shape[1]
    nt = s // tm
    row = lambda w: pl.BlockSpec((tm, w), lambda bi, i: (bi * nt + i, 0))
    return pl.pallas_call(
        _merge_kernel,
        out_shape=jax.ShapeDtypeStruct((t, d), F32),
        grid=(b, nt),
        in_specs=[row(d), pl.BlockSpec((1, at.shape[1], tm), lambda bi, i: (bi, 0, i)),
                  row(mw), row(mw), row(mw), row(mw), row(d), row(d)]
                 + [_resident(a.shape) for a in (mnw, skip, wa, wb, wo, nw, wg, wu, wd, fw)],
        out_specs=row(d),
        compiler_params=_params("parallel", "parallel"),
        name="merge",
    )(x, at, hf, hb, xc, opre, ga, gb, mnw, skip, wa, wb, wo, nw, wg, wu, wd, fw)


def _pad_cols(w, width):
    return jnp.pad(w, ((0, 0), (0, width - w.shape[1])))


def _head_pad(w, head_w):
    r = w.shape[0]
    w = w.reshape(r, MLA_HEADS, head_w)
    return jnp.pad(w, ((0, 0), (0, 0), (0, LANES - head_w))).reshape(r, MLA_HEADS * LANES)


def kernel(x, positions, ffn1_norm_w, ffn1_w_gate, ffn1_w_up, ffn1_w_down, mix_norm_w, w_in, q_a_norm_w, w_uq, kv_a_norm_w, w_uk, w_uv, q_norm_w, k_norm_w, w_branch_a, conv_w, conv_b, w_mq, w_mk, w_mv, b_igate, b_fgate, m_norm_w, m_skip, w_branch_b, w_out, ffn2_norm_w, ffn2_w_gate, ffn2_w_up, ffn2_w_down, final_norm_w):
    b, s, d = x.shape
    depth = w_in.shape[0]
    q_lora, kv_lora = q_a_norm_w.shape[1], kv_a_norm_w.shape[1]
    mw = m_norm_w.shape[1]
    bf = lambda a: a.astype(BF16)
    row = lambda a: a.reshape(1, -1).astype(F32)

    half = QK_ROPE // 2
    inv = (ROPE_THETA ** (-jnp.arange(half, dtype=F32) / half)).reshape(half, 1)
    q_scale = QK_HEAD ** -0.5 * math.log2(math.e)

    xt = x.reshape(b * s, d)
    pos = positions.reshape(1, b * s)
    for l in range(depth):
        xt = _ffn(xt, row(ffn1_norm_w[l]), bf(ffn1_w_gate[l]), bf(ffn1_w_up[l]), bf(ffn1_w_down[l]))

        w = w_in[l]
        o_lat = q_lora + kv_lora + QK_ROPE
        o_g = o_lat + 2 * mw
        o_ga = o_g + 4 * M_HEADS
        wcat = jnp.concatenate(
            [_pad_cols(w[:, :o_lat], LAT_W), w[:, o_lat:o_lat + mw], w[:, o_lat + mw:o_g],
             w[:, o_ga:o_ga + d], w[:, o_ga + d:]], axis=1)
        gbias = jnp.concatenate([b_igate[l], b_fgate[l]]).reshape(-1, 1).astype(F32)
        lat, m_in, o_pre, g_a, g_b, gt = _proj(xt, row(mix_norm_w[l]), bf(wcat), bf(w[:, o_g:o_ga].T), gbias,
                                               mw=mw)

        qht, kh, vt = _mla_prep(
            lat, pos, row(q_a_norm_w[l]), bf(_head_pad(w_uq[l], QK_HEAD).T), row(kv_a_norm_w[l]),
            bf(_head_pad(w_uk[l], QK_NOPE)), bf(w_uv[l].T), _pad_cols(row(q_norm_w[l]), LANES).T,
            _pad_cols(row(k_norm_w[l]), LANES), inv,
            b=b, s=s, q_lora=q_lora, kv_lora=kv_lora, q_scale=q_scale)
        at = _attn(qht, kh, vt).reshape(b, MLA_HEADS * V_HEAD, s)

        xc, mqt, mk, mvt = _mprep(m_in, conv_w[l], row(conv_b[l]), bf(jnp.swapaxes(w_mq[l], 1, 2)), bf(w_mk[l]),
                                  bf(jnp.swapaxes(w_mv[l], 1, 2)), b=b, s=s)
        hf, hb = _mscan(mqt, mk, mvt, gt, b=b, s=s)

        xt = _merge(xt, at, hf, hb, xc, o_pre, g_a, g_b, row(m_norm_w[l]), row(m_skip[l]),
                    bf(w_branch_a[l]), bf(w_branch_b[l]), bf(w_out[l]), row(ffn2_norm_w[l]), bf(ffn2_w_gate[l]),
                    bf(ffn2_w_up[l]), bf(ffn2_w_down[l]), row(final_norm_w[l]), b=b, s=s)
    return xt.reshape(b, s, d)
```

```python
import functools
import math

import jax
import jax.numpy as jnp
from jax import lax
from jax.experimental import pallas as pl
from jax.experimental.pallas import tpu as pltpu

F32 = jnp.float32
BF16 = jnp.bfloat16

EPS = 1e-6
ROPE_THETA = 10000.0
LANES = 128
MLA_HEADS = 8
QK_NOPE = 64
QK_ROPE = 32
QK_HEAD = QK_NOPE + QK_ROPE
V_HEAD = 64
VT_ROWS = 80
M_HEADS = 4
M_HEAD = 128
CONV_K = 5
CHUNK = 512
CT_ROWS = 144
LAT_W = 512
VMEM_LIMIT = 56 * 1024 * 1024

_NT = (((1,), (1,)), ((), ()))
_TN = (((0,), (0,)), ((), ()))


def _rms(x, w):
    ms = jnp.mean(x * x, axis=-1, keepdims=True)
    return x * lax.rsqrt(ms + EPS) * w


def _dot(a, b):
    return jnp.dot(a, b, preferred_element_type=F32)


def _resident(shape):
    nd = len(shape)
    return pl.BlockSpec(shape, lambda *_: (0,) * nd, pipeline_mode=pl.Buffered(1))


def _params(*sem):
    return pltpu.CompilerParams(dimension_semantics=sem, vmem_limit_bytes=VMEM_LIMIT)


def _half_step(x, nw_ref, wg_ref, wu_ref, wd_ref):
    h = _rms(x, nw_ref[...]).astype(BF16)
    g = _dot(h, wg_ref[...])
    u = _dot(h, wu_ref[...])
    a = (g * jax.nn.sigmoid(g) * u).astype(BF16)
    return x + 0.5 * _dot(a, wd_ref[...])


def _ffn_kernel(x_ref, nw_ref, wg_ref, wu_ref, wd_ref, o_ref):
    o_ref[...] = _half_step(x_ref[...], nw_ref, wg_ref, wu_ref, wd_ref)


def _ffn(x, nw, wg, wu, wd, *, tm=512):
    t, d = x.shape
    f = wg.shape[1]
    row = pl.BlockSpec((tm, d), lambda i: (i, 0))
    return pl.pallas_call(
        _ffn_kernel,
        out_shape=jax.ShapeDtypeStruct((t, d), F32),
        grid=(t // tm,),
        in_specs=[row, _resident((1, d)), _resident((d, f)), _resident((d, f)), _resident((f, d))],
        out_specs=row,
        compiler_params=_params("parallel"),
        name="ffn",
    )(x, nw, wg, wu, wd)


def _proj_kernel(x_ref, nw_ref, w_ref, wgt_ref, gbias_ref, lat_ref, min_ref, opre_ref, ga_ref, gb_ref, gt_ref,
                 *, d, mw):
    h = _rms(x_ref[...], nw_ref[...]).astype(BF16)
    o = 0
    for ref, width in ((lat_ref, LAT_W), (min_ref, mw), (opre_ref, mw), (ga_ref, d), (gb_ref, d)):
        ref[...] = _dot(h, w_ref[:, o:o + width]).astype(ref.dtype)
        o += width
    gt_ref[...] = lax.dot_general(wgt_ref[...], h, _NT, preferred_element_type=F32) + gbias_ref[...]


def _proj(x, nw, wcat, wgt, gbias, *, mw, tm=512):
    t, d = x.shape
    n = wcat.shape[1]
    ng = wgt.shape[0]
    row = lambda w: pl.BlockSpec((tm, w), lambda i: (i, 0))
    out_shape = (jax.ShapeDtypeStruct((t, LAT_W), F32), jax.ShapeDtypeStruct((t, mw), F32),
                 jax.ShapeDtypeStruct((t, mw), BF16), jax.ShapeDtypeStruct((t, d), BF16),
                 jax.ShapeDtypeStruct((t, d), BF16), jax.ShapeDtypeStruct((ng, t), F32))
    return pl.pallas_call(
        functools.partial(_proj_kernel, d=d, mw=mw),
        out_shape=out_shape,
        grid=(t // tm,),
        in_specs=[row(d), _resident((1, d)), _resident((d, n)), _resident((ng, d)), _resident((ng, 1))],
        out_specs=(row(LAT_W), row(mw), row(mw), row(d), row(d), pl.BlockSpec((ng, tm), lambda i: (0, i))),
        compiler_params=_params("parallel"),
        name="proj",
    )(x, nw, wcat, wgt, gbias)


def _mla_prep_kernel(lat_ref, pos_ref, qaw_ref, wuqt_ref, kvaw_ref, wuk_ref, wuvt_ref, qnw_ref, knw_ref,
                     inv_ref, qt_ref, k_ref, vt_ref, *, q_lora, kv_lora, q_scale):
    tm = lat_ref.shape[0]
    lat = lat_ref[...]
    cqn = _rms(lat[:, 0:q_lora], qaw_ref[...]).astype(BF16)
    ckvn = _rms(lat[:, q_lora:q_lora + kv_lora], kvaw_ref[...]).astype(BF16)
    kpe = pltpu.roll(lat[:, q_lora + kv_lora:LAT_W], QK_NOPE, axis=1)
    qft = lax.dot_general(wuqt_ref[...], cqn, _NT, preferred_element_type=F32)
    kf = _dot(ckvn, wuk_ref[...])
    vt = lax.dot_general(wuvt_ref[...], ckvn, _NT, preferred_element_type=F32)

    ang = inv_ref[...] * pos_ref[...].astype(F32)
    cos_t, sin_t = jnp.cos(ang), jnp.sin(ang)
    fill = lambda rows, v: jnp.full((rows, tm), v, F32)
    cos = jnp.concatenate([fill(QK_NOPE, 1.0), cos_t, cos_t, fill(LANES - QK_HEAD, 1.0)], axis=0).T
    sin = jnp.concatenate([fill(QK_NOPE, 0.0), -sin_t, sin_t, fill(LANES - QK_HEAD, 0.0)], axis=0).T
    lane = lax.broadcasted_iota(jnp.int32, (tm, LANES), 1)
    first_half = lane < QK_NOPE + QK_ROPE // 2

    def norm_rope(x, w):
        ms = jnp.sum(x * x, axis=-1, keepdims=True) * (1.0 / QK_HEAD)
        y = x * lax.rsqrt(ms + EPS) * w
        partner = jnp.where(first_half, pltpu.roll(y, LANES - QK_ROPE // 2, axis=1),
                            pltpu.roll(y, QK_ROPE // 2, axis=1))
        return y * cos + partner * sin

    def norm_rope_t(xt, w):
        ms = jnp.sum(xt * xt, axis=0, keepdims=True) * (1.0 / QK_HEAD)
        y = xt * lax.rsqrt(ms + EPS) * w
        y1, y2 = y[QK_NOPE:QK_NOPE + QK_ROPE // 2], y[QK_NOPE + QK_ROPE // 2:QK_HEAD]
        return jnp.concatenate([y[0:QK_NOPE], y1 * cos_t - y2 * sin_t, y2 * cos_t + y1 * sin_t, y[QK_HEAD:]], axis=0)

    ones_row = jnp.where(lax.broadcasted_iota(jnp.int32, (VT_ROWS - V_HEAD, tm), 0) == 0, 1.0, 0.0).astype(BF16)
    for h in range(MLA_HEADS):
        sl = slice(h * LANES, (h + 1) * LANES)
        qt_ref[0, h] = (norm_rope_t(qft[sl, :], qnw_ref[...]) * q_scale).astype(BF16)
        k_ref[0, h] = norm_rope(kf[:, sl] + kpe, knw_ref[...]).astype(BF16)
        vt_ref[0, h, 0:V_HEAD, :] = vt[h * V_HEAD:(h + 1) * V_HEAD, :].astype(BF16)
        vt_ref[0, h, V_HEAD:VT_ROWS, :] = ones_row


def _mla_prep(lat, pos, qaw, wuqt, kvaw, wuk, wuvt, qnw, knw, inv, *, b, s, q_lora, kv_lora, q_scale, tm=512):
    nt = s // tm
    hq = MLA_HEADS
    row = lambda w: pl.BlockSpec((tm, w), lambda bi, i: (bi * nt + i, 0))
    qk_spec = pl.BlockSpec((1, hq, tm, LANES), lambda bi, i: (bi, 0, i, 0))
    in_specs = [row(LAT_W), pl.BlockSpec((1, tm), lambda bi, i: (0, bi * nt + i))] + [
        _resident(a.shape) for a in (qaw, wuqt, kvaw, wuk, wuvt, qnw, knw, inv)]
    return pl.pallas_call(
        functools.partial(_mla_prep_kernel, q_lora=q_lora, kv_lora=kv_lora, q_scale=q_scale),
        out_shape=(jax.ShapeDtypeStruct((b, hq, LANES, s), BF16), jax.ShapeDtypeStruct((b, hq, s, LANES), BF16),
                   jax.ShapeDtypeStruct((b, hq, VT_ROWS, s), BF16)),
        grid=(b, nt),
        in_specs=in_specs,
        out_specs=(pl.BlockSpec((1, hq, LANES, tm), lambda bi, i: (bi, 0, 0, i)), qk_spec,
                   pl.BlockSpec((1, hq, VT_ROWS, tm), lambda bi, i: (bi, 0, 0, i))),
        compiler_params=_params("parallel", "parallel"),
        name="mla_prep",
    )(lat, pos, qaw, wuqt, kvaw, wuk, wuvt, qnw, knw, inv)


def _attn_kernel(qt_ref, k_ref, vt_ref, o_ref, s0_scr, s1_scr, *, tk):
    tq = qt_ref.shape[3]
    nk = k_ref.shape[2] // tk
    qt = qt_ref[0, 0]

    def chunk(c):
        return pl.ds(pl.multiple_of(c * tk, tk), tk)

    def scores(c, s_scr):
        st = _dot(k_ref[0, 0, chunk(c), :], qt)
        s_scr[...] = st
        return jnp.max(st, axis=0, keepdims=True)

    def values(c, s_scr, m, cm, acc):
        m_new = jnp.maximum(m, cm)
        p = jnp.exp2(s_scr[...] - m_new).astype(BF16)
        return m_new, acc * jnp.exp2(m - m_new) + _dot(vt_ref[0, 0, :, chunk(c)], p)

    def pair(i, carry):
        m, cm, acc = carry
        cm1 = scores(2 * i + 1, s1_scr)
        m, acc = values(2 * i, s0_scr, m, cm, acc)
        cm0 = scores(2 * i + 2, s0_scr)
        m, acc = values(2 * i + 1, s1_scr, m, cm1, acc)
        return m, cm0, acc

    init = (jnp.full((1, tq), -jnp.inf, F32), scores(0, s0_scr), jnp.zeros((VT_ROWS, tq), F32))
    m, cm, acc = lax.fori_loop(0, nk // 2 - 1, pair, init)
    cm1 = scores(nk - 1, s1_scr)
    m, acc = values(nk - 2, s0_scr, m, cm, acc)
    m, acc = values(nk - 1, s1_scr, m, cm1, acc)
    o_ref[0, 0] = (acc[0:V_HEAD] / acc[V_HEAD:V_HEAD + 1]).astype(o_ref.dtype)


def _attn(qt, k, vt, *, tq=4096, tk=1024):
    b, h, s, _ = k.shape
    assert s % (2 * tk) == 0 and s % tq == 0
    return pl.pallas_call(
        functools.partial(_attn_kernel, tk=tk),
        out_shape=jax.ShapeDtypeStruct((b, h, V_HEAD, s), BF16),
        grid=(b, h, s // tq),
        in_specs=[pl.BlockSpec((1, 1, LANES, tq), lambda bi, hi, i: (bi, hi, 0, i)),
                  pl.BlockSpec((1, 1, s, LANES), lambda bi, hi, i: (bi, hi, 0, 0)),
                  pl.BlockSpec((1, 1, VT_ROWS, s), lambda bi, hi, i: (bi, hi, 0, 0))],
        out_specs=pl.BlockSpec((1, 1, V_HEAD, tq), lambda bi, hi, i: (bi, hi, 0, i)),
        scratch_shapes=[pltpu.VMEM((tk, tq), F32), pltpu.VMEM((tk, tq), F32)],
        compiler_params=_params("parallel", "parallel", "arbitrary"),
        name="attn",
    )(qt, k, vt)


HALO = 8


def _mprep_kernel(cur_ref, prev_ref, next_ref, cw_ref, cb_ref, wqt_ref, wk_ref, wvt_ref,
                  xc_ref, qt_ref, k_ref, vt_ref, xs_ref):
    tm = cur_ref.shape[0]
    i = pl.program_id(1)
    cur = cur_ref[...]
    xs_ref[0:HALO, :] = jnp.where(i > 0, prev_ref[...], 0.0)
    xs_ref[HALO:HALO + tm, :] = cur
    xs_ref[HALO + tm:2 * HALO + tm, :] = jnp.where(i < pl.num_programs(1) - 1, next_ref[...], 0.0)
    y = cb_ref[...]
    for kk in range(CONV_K):
        y = y + cw_ref[kk:kk + 1, :] * xs_ref[pl.ds(HALO - CONV_K // 2 + kk, tm), :]
    xc = (y * jax.nn.sigmoid(y)).astype(BF16)
    xc_ref[...] = xc
    xm = cur.astype(BF16)
    for h in range(M_HEADS):
        sl = slice(h * M_HEAD, (h + 1) * M_HEAD)
        qt_ref[sl, :] = lax.dot_general(wqt_ref[h], xc[:, sl], _NT, preferred_element_type=F32).astype(BF16)
        k_ref[:, sl] = (_dot(xc[:, sl], wk_ref[h]) * (M_HEAD ** -0.5)).astype(BF16)
        vt_ref[sl, :] = lax.dot_general(wvt_ref[h], xm[:, sl], _NT, preferred_element_type=F32).astype(BF16)


def _mprep(m_in, cw, cb, wqt, wk, wvt, *, b, s, tm=512):
    t, mw = m_in.shape
    nt = s // tm
    hb = tm // HALO
    row = pl.BlockSpec((tm, mw), lambda bi, i: (bi * nt + i, 0))
    prev = pl.BlockSpec((HALO, mw), lambda bi, i: (jnp.maximum((bi * nt + i) * hb - 1, 0), 0))
    nxt = pl.BlockSpec((HALO, mw), lambda bi, i: (jnp.minimum((bi * nt + i + 1) * hb, t // HALO - 1), 0))
    col = pl.BlockSpec((mw, tm), lambda bi, i: (0, bi * nt + i))
    out = jax.ShapeDtypeStruct((t, mw), BF16)
    out_t = jax.ShapeDtypeStruct((mw, t), BF16)
    return pl.pallas_call(
        _mprep_kernel,
        out_shape=(out, out_t, out, out_t),
        grid=(b, nt),
        in_specs=[row, prev, nxt] + [_resident(a.shape) for a in (cw, cb, wqt, wk, wvt)],
        out_specs=(row, col, row, col),
        scratch_shapes=[pltpu.VMEM((tm + 2 * HALO, mw), F32)],
        compiler_params=_params("parallel", "parallel"),
        name="m_prep",
    )(m_in, m_in, m_in, cw, cb, wqt, wk, wvt)


def _log_sigmoid(x):
    return jnp.minimum(x, 0.0) - jnp.log1p(jnp.exp(-jnp.abs(x)))


def _cumsum_lanes(x, tri):
    n = x.shape[0]
    hi = x.astype(BF16)
    r1 = x - hi.astype(F32)
    mid = r1.astype(BF16)
    lo = (r1 - mid.astype(F32)).astype(BF16)
    y = _dot(jnp.concatenate([hi, mid, lo], axis=0), tri)
    return y[0:n] + y[n:2 * n] + y[2 * n:3 * n]


def _mscan_kernel(qf_ref, kf_ref, vf_ref, gf_ref, qb_ref, kb_ref, vb_ref, gb_ref,
                  hf_ref, hb_ref, c_scr, m_scr, *, sub):
    L = CHUNK
    ng = 2 * M_HEADS

    @pl.when(pl.program_id(1) == 0)
    def _():
        c_scr[...] = jnp.zeros_like(c_scr)
        m_scr[...] = jnp.zeros_like(m_scr)

    row = lax.broadcasted_iota(jnp.int32, (L, L), 0)
    col = lax.broadcasted_iota(jnp.int32, (L, L), 1)
    visible = (row <= col, row >= col)
    tri = tuple(jnp.where(v, 1.0, 0.0).astype(BF16) for v in visible)
    ones_rows = jnp.where(lax.broadcasted_iota(jnp.int32, (CT_ROWS - M_HEAD, L), 0) == 0, 1.0, 0.0).astype(BF16)
    total = (L - 1, 0)
    dirs = ((qf_ref, kf_ref, vf_ref, gf_ref, hf_ref), (qb_ref, kb_ref, vb_ref, gb_ref, hb_ref))

    for c in range(sub):
        for d, (q_ref, k_ref, v_ref, g_ref, h_ref) in enumerate(dirs):
            lo = (c if d == 0 else sub - 1 - c) * L
            tok = slice(lo, lo + L)
            gt = g_ref[:, tok]
            bt = _cumsum_lanes(_log_sigmoid(gt), tri[d])
            r8 = gt[0:ng] - bt[ng:2 * ng]
            rc = jnp.concatenate([r8, jnp.zeros((LANES - ng, L), F32)], axis=0).T
            for h in range(M_HEADS):
                ch = d * M_HEADS + h
                hs = slice(h * M_HEAD, (h + 1) * M_HEAD)
                qt, kc = q_ref[hs, tok], k_ref[tok, hs]
                v_ext = jnp.concatenate([v_ref[hs, tok], ones_rows], axis=0)
                b_row = bt[ng + ch:ng + ch + 1, :]
                r_row = r8[ch:ch + 1, :]
                b_tot = b_row[:, total[d]:total[d] + 1]
                m = m_scr[ch, 0:1, 0:1]
                cst = c_scr[ch]

                logw = jnp.where(visible[d], b_row + rc[:, ch:ch + 1], -jnp.inf)
                mx = jnp.max(logw, axis=0, keepdims=True)
                kq = _dot(jnp.concatenate([kc, cst.astype(BF16)], axis=0), qt)
                intra = _dot(v_ext, (kq[0:L] * jnp.exp(logw - mx)).astype(BF16))
                g_row = b_tot + r_row
                g_max = jnp.max(g_row, axis=1, keepdims=True)
                incr = _dot((v_ext.astype(F32) * jnp.exp(g_row - g_max)).astype(BF16), kc)

                inter = b_row + m
                m_t = jnp.maximum(mx, inter)
                tot = jnp.exp(mx - m_t) * intra + jnp.exp(inter - m_t) * kq[L:L + CT_ROWS]
                den = jnp.maximum(jnp.abs(tot[M_HEAD:M_HEAD + 1]), jnp.exp(-m_t))
                h_ref[tok, hs] = (tot[0:M_HEAD] * (1.0 / den)).T.astype(h_ref.dtype)
                m_new = jnp.maximum(b_tot + m, g_max)
                c_scr[ch] = jnp.exp(b_tot + m - m_new) * cst + jnp.exp(g_max - m_new) * incr
                m_scr[ch] = jnp.broadcast_to(m_new, m_scr.shape[1:])


def _mscan(qt, k, vt, gt, *, b, s, sub=2):
    t, mw = k.shape
    ng = gt.shape[0]
    blk = sub * CHUNK
    nb = s // blk
    fwd = pl.BlockSpec((blk, mw), lambda bi, j: (bi * nb + j, 0))
    bwd = pl.BlockSpec((blk, mw), lambda bi, j: (bi * nb + nb - 1 - j, 0))
    fwd_t = lambda r: pl.BlockSpec((r, blk), lambda bi, j: (0, bi * nb + j))
    bwd_t = lambda r: pl.BlockSpec((r, blk), lambda bi, j: (0, bi * nb + nb - 1 - j))
    out = jax.ShapeDtypeStruct((t, mw), BF16)
    return pl.pallas_call(
        functools.partial(_mscan_kernel, sub=sub),
        out_shape=(out, out),
        grid=(b, nb),
        in_specs=[fwd_t(mw), fwd, fwd_t(mw), fwd_t(ng), bwd_t(mw), bwd, bwd_t(mw), bwd_t(ng)],
        out_specs=(fwd, bwd),
        scratch_shapes=[pltpu.VMEM((2 * M_HEADS, CT_ROWS, M_HEAD), F32),
                        pltpu.VMEM((2 * M_HEADS, 8, LANES), F32)],
        compiler_params=_params("parallel", "arbitrary"),
        name="m_scan",
    )(qt, k, vt, gt, qt, k, vt, gt)


def _merge_kernel(x_ref, at_ref, hf_ref, hb_ref, xc_ref, opre_ref, ga_ref, gb_ref,
                  mnw_ref, skip_ref, wa_ref, wb_ref, wo_ref, nw_ref, wg_ref, wu_ref, wd_ref, fw_ref, o_ref):
    hc = hf_ref[...].astype(F32) + hb_ref[...].astype(F32)
    hn = jnp.concatenate(
        [_rms(hc[:, h * M_HEAD:(h + 1) * M_HEAD], mnw_ref[:, h * M_HEAD:(h + 1) * M_HEAD])
         for h in range(M_HEADS)], axis=1)
    bm = jax.nn.sigmoid(opre_ref[...].astype(F32)) * (hn + skip_ref[...] * xc_ref[...].astype(F32))
    ya = lax.dot_general(at_ref[0], wa_ref[...], _TN, preferred_element_type=F32)
    yb = _dot(bm.astype(BF16), wb_ref[...])
    merged = (jax.nn.sigmoid(ga_ref[...].astype(F32)) * ya + jax.nn.sigmoid(gb_ref[...].astype(F32)) * yb)
    x = x_ref[...] + _dot(merged.astype(BF16), wo_ref[...])
    o_ref[...] = _rms(_half_step(x, nw_ref, wg_ref, wu_ref, wd_ref), fw_ref[...])


def _merge(x, at, hf, hb, xc, opre, ga, gb, mnw, skip, wa, wb, wo, nw, wg, wu, wd, fw, *, b, s, tm=512):
    t, d = x.shape
    mw = hf.shape[1]
    nt = s // tm
    row = lambda w: pl.BlockSpec((tm, w), lambda bi, i: (bi * nt + i, 0))
    return pl.pallas_call(
        _merge_kernel,
        out_shape=jax.ShapeDtypeStruct((t, d), F32),
        grid=(b, nt),
        in_specs=[row(d), pl.BlockSpec((1, at.shape[1], tm), lambda bi, i: (bi, 0, i)),
                  row(mw), row(mw), row(mw), row(mw), row(d), row(d)]
                 + [_resident(a.shape) for a in (mnw, skip, wa, wb, wo, nw, wg, wu, wd, fw)],
        out_specs=row(d),
        compiler_params=_params("parallel", "parallel"),
        name="merge",
    )(x, at, hf, hb, xc, opre, ga, gb, mnw, skip, wa, wb, wo, nw, wg, wu, wd, fw)


def _pad_cols(w, width):
    return jnp.pad(w, ((0, 0), (0, width - w.shape[1])))


def _head_pad(w, head_w):
    r = w.shape[0]
    w = w.reshape(r, MLA_HEADS, head_w)
    return jnp.pad(w, ((0, 0), (0, 0), (0, LANES - head_w))).reshape(r, MLA_HEADS * LANES)


def kernel(x, positions, ffn1_norm_w, ffn1_w_gate, ffn1_w_up, ffn1_w_down, mix_norm_w, w_in, q_a_norm_w, w_uq, kv_a_norm_w, w_uk, w_uv, q_norm_w, k_norm_w, w_branch_a, conv_w, conv_b, w_mq, w_mk, w_mv, b_igate, b_fgate, m_norm_w, m_skip, w_branch_b, w_out, ffn2_norm_w, ffn2_w_gate, ffn2_w_up, ffn2_w_down, final_norm_w):
    b, s, d = x.shape
    depth = w_in.shape[0]
    q_lora, kv_lora = q_a_norm_w.shape[1], kv_a_norm_w.shape[1]
    mw = m_norm_w.shape[1]
    bf = lambda a: a.astype(BF16)
    row = lambda a: a.reshape(1, -1).astype(F32)

    half = QK_ROPE // 2
    inv = (ROPE_THETA ** (-jnp.arange(half, dtype=F32) / half)).reshape(half, 1)
    q_scale = QK_HEAD ** -0.5 * math.log2(math.e)

    xt = x.reshape(b * s, d)
    pos = positions.reshape(1, b * s)
    for l in range(depth):
        xt = _ffn(xt, row(ffn1_norm_w[l]), bf(ffn1_w_gate[l]), bf(ffn1_w_up[l]), bf(ffn1_w_down[l]))

        w = w_in[l]
        o_lat = q_lora + kv_lora + QK_ROPE
        o_g = o_lat + 2 * mw
        o_ga = o_g + 4 * M_HEADS
        wcat = jnp.concatenate(
            [_pad_cols(w[:, :o_lat], LAT_W), w[:, o_lat:o_lat + mw], w[:, o_lat + mw:o_g],
             w[:, o_ga:o_ga + d], w[:, o_ga + d:]], axis=1)
        gbias = jnp.concatenate([b_igate[l], b_fgate[l]]).reshape(-1, 1).astype(F32)
        lat, m_in, o_pre, g_a, g_b, gt = _proj(xt, row(mix_norm_w[l]), bf(wcat), bf(w[:, o_g:o_ga].T), gbias,
                                               mw=mw)

        qht, kh, vt = _mla_prep(
            lat, pos, row(q_a_norm_w[l]), bf(_head_pad(w_uq[l], QK_HEAD).T), row(kv_a_norm_w[l]),
            bf(_head_pad(w_uk[l], QK_NOPE)), bf(w_uv[l].T), _pad_cols(row(q_norm_w[l]), LANES).T,
            _pad_cols(row(k_norm_w[l]), LANES), inv,
            b=b, s=s, q_lora=q_lora, kv_lora=kv_lora, q_scale=q_scale)
        at = _attn(qht, kh, vt).reshape(b, MLA_HEADS * V_HEAD, s)

        xc, mqt, mk, mvt = _mprep(m_in, conv_w[l], row(conv_b[l]), bf(jnp.swapaxes(w_mq[l], 1, 2)), bf(w_mk[l]),
                                  bf(jnp.swapaxes(w_mv[l], 1, 2)), b=b, s=s)
        hf, hb = _mscan(mqt, mk, mvt, gt, b=b, s=s)

        xt = _merge(xt, at, hf, hb, xc, o_pre, g_a, g_b, row(m_norm_w[l]), row(m_skip[l]),
                    bf(w_branch_a[l]), bf(w_branch_b[l]), bf(w_out[l]), row(ffn2_norm_w[l]), bf(ffn2_w_gate[l]),
                    bf(ffn2_w_up[l]), bf(ffn2_w_down[l]), row(final_norm_w[l]), b=b, s=s)
    return xt.reshape(b, s, d)
```
